```python
import jax, jax.numpy as jnp
from jax import lax
import numpy as np

D_MODEL = 1024
BATCH = 8
SEQ = 4096
DEPTH = 4

N_MIXERS = 3
SHORT_CONV_K = 3
CONFORMER_K = 31
POOL_WINDOWS = (2, 4, 8, 16)
POOL_GROUP = D_MODEL // len(POOL_WINDOWS)
D_FF = ((8 * D_MODEL // 3 + 255) // 256) * 256
N_A = len(range(0, DEPTH, N_MIXERS))
N_B = len(range(1, DEPTH, N_MIXERS))
N_C = len(range(2, DEPTH, N_MIXERS))
EPS = 1e-6

kernel_name = "hybrid_conv_pool_interleaved_adaln"


def rmsnorm(x, g):
    xf = x.astype(jnp.float32)
    y = xf * lax.rsqrt(jnp.mean(xf * xf, axis=-1, keepdims=True) + EPS)
    return (y * g.astype(jnp.float32)).astype(x.dtype)


def layernorm(x, g, b):
    xf = x.astype(jnp.float32)
    mu = jnp.mean(xf, axis=-1, keepdims=True)
    var = jnp.mean(jnp.square(xf - mu), axis=-1, keepdims=True)
    y = (xf - mu) * lax.rsqrt(var + EPS)
    return (y * g.astype(jnp.float32) + b.astype(jnp.float32)).astype(x.dtype)


def causal_dwconv(u, w):
    k = w.shape[0]
    return lax.conv_general_dilated(
        u, w[:, None, :].astype(u.dtype), window_strides=(1,), padding=[(k - 1, 0)],
        dimension_numbers=("NWC", "WIO", "NWC"), feature_group_count=u.shape[-1])


def short_gated_conv(h, w_in, conv_w, w_out):
    proj = jnp.einsum("bsd,de->bse", h, w_in)
    gb, gc, hv = jnp.split(proj, 3, axis=-1)
    v = causal_dwconv(gc * hv, conv_w)
    return jnp.einsum("bsd,de->bse", gb * v, w_out)


def conformer_conv(h, w_pw1, b_pw1, conv_w, conv_b, ln_g, ln_b, w_pw2, b_pw2):
    p = jnp.einsum("bsd,de->bse", h, w_pw1) + b_pw1
    a, gt = jnp.split(p, 2, axis=-1)
    u = a * jax.nn.sigmoid(gt)
    v = causal_dwconv(u, conv_w) + conv_b
    v = jax.nn.silu(layernorm(v, ln_g, ln_b))
    return jnp.einsum("bsd,de->bse", v, w_pw2) + b_pw2


def multiscale_pool(h, w_grp, scale):
    bsz, s, d = h.shape
    hf = h.astype(jnp.float32)
    cs_pad = jnp.concatenate([jnp.zeros((bsz, 1, d), jnp.float32), jnp.cumsum(hf, axis=1)], axis=1)
    t = jnp.arange(s)
    outs = []
    for g, w in enumerate(POOL_WINDOWS):
        sl = slice(g * POOL_GROUP, (g + 1) * POOL_GROUP)
        cg = cs_pad[:, :, sl]
        prev = jnp.concatenate([jnp.zeros((bsz, w - 1, POOL_GROUP), jnp.float32), cg[:, : s - w + 1]], axis=1)
        cnt = jnp.minimum(t + 1, w).astype(jnp.float32)[None, :, None]
        pooled = (cg[:, 1:] - prev) / cnt - hf[:, :, sl]
        outs.append(jnp.einsum("bsg,gh->bsh", pooled.astype(h.dtype), w_grp[g]))
    return jnp.concatenate(outs, axis=-1) * scale


def swiglu(h, w_in, w_out):
    gate, up = jnp.split(jnp.einsum("bsd,df->bsf", h, w_in), 2, axis=-1)
    return jnp.einsum("bsf,fd->bsd", jax.nn.silu(gate) * up, w_out)


def setup_inputs(seed: int = 0) -> dict:
    key = jax.random.key(seed)
    ks = jax.random.split(key, 24)
    D, F, G = D_MODEL, D_FF, POOL_GROUP
    nrm = lambda k, shape, s: jax.random.normal(k, shape, jnp.float32) * s
    return {
        "x": nrm(ks[0], (BATCH, SEQ, D), 1.0),
        "c": nrm(ks[1], (BATCH, D), 1.0),
        "ada_w": nrm(ks[2], (DEPTH, D, 6 * D), 0.5 * D ** -0.5),
        "ada_b": nrm(ks[3], (DEPTH, 6 * D), 0.02),
        "norm_mix_g": 1.0 + nrm(ks[4], (DEPTH, D), 0.02),
        "norm_ffn_g": 1.0 + nrm(ks[5], (DEPTH, D), 0.02),
        "a_w_in": nrm(ks[6], (N_A, D, 3 * D), D ** -0.5),
        "a_conv_w": nrm(ks[7], (N_A, SHORT_CONV_K, D), SHORT_CONV_K ** -0.5),
        "a_w_out": nrm(ks[8], (N_A, D, D), D ** -0.5),
        "b_w_pw1": nrm(ks[9], (N_B, D, 2 * D), D ** -0.5),
        "b_b_pw1": nrm(ks[10], (N_B, 2 * D), 0.02),
        "b_conv_w": nrm(ks[11], (N_B, CONFORMER_K, D), CONFORMER_K ** -0.5),
        "b_conv_b": nrm(ks[12], (N_B, D), 0.02),
        "b_ln_g": 1.0 + nrm(ks[13], (N_B, D), 0.02),
        "b_ln_b": nrm(ks[14], (N_B, D), 0.02),
        "b_w_pw2": nrm(ks[15], (N_B, D, D), D ** -0.5),
        "b_b_pw2": nrm(ks[16], (N_B, D), 0.02),
        "p_w_grp": nrm(ks[17], (N_C, len(POOL_WINDOWS), G, G), G ** -0.5),
        "p_scale": 1.0 + nrm(ks[18], (N_C, D), 0.1),
        "ffn_w_in": nrm(ks[19], (DEPTH, D, 2 * F), D ** -0.5),
        "ffn_w_out": nrm(ks[20], (DEPTH, F, D), F ** -0.5),
        "final_g": 1.0 + nrm(ks[21], (D,), 0.02),
    }


def reference(x, c, ada_w, ada_b, norm_mix_g, norm_ffn_g, a_w_in, a_conv_w, a_w_out,
              b_w_pw1, b_b_pw1, b_conv_w, b_conv_b, b_ln_g, b_ln_b, b_w_pw2, b_b_pw2,
              p_w_grp, p_scale, ffn_w_in, ffn_w_out, final_g):
    c_act = jax.nn.silu(c)
    for i in range(DEPTH):
        kind, j = i % N_MIXERS, i // N_MIXERS
        mod = jnp.einsum("bd,de->be", c_act, ada_w[i]) + ada_b[i]
        sh1, sc1, g1, sh2, sc2, g2 = [m[:, None, :] for m in jnp.split(mod, 6, axis=-1)]
        h = rmsnorm(x, norm_mix_g[i]) * (1.0 + sc1) + sh1
        if kind == 0:
            y = short_gated_conv(h, a_w_in[j], a_conv_w[j], a_w_out[j])
        elif kind == 1:
            y = conformer_conv(h, b_w_pw1[j], b_b_pw1[j], b_conv_w[j], b_conv_b[j],
                               b_ln_g[j], b_ln_b[j], b_w_pw2[j], b_b_pw2[j])
        else:
            y = multiscale_pool(h, p_w_grp[j], p_scale[j])
        x = x + g1 * y
        h = rmsnorm(x, norm_ffn_g[i]) * (1.0 + sc2) + sh2
        x = x + g2 * swiglu(h, ffn_w_in[i], ffn_w_out[i])
    return rmsnorm(x, final_g)
```

```python
import functools

import jax
import jax.numpy as jnp
from jax import lax
from jax.experimental import pallas as pl
from jax.experimental.pallas import tpu as pltpu

EPS = 1e-6
POOL_WINDOWS = (2, 4, 8, 16)
N_MIXERS = 3
N_MOD = 6

V7X_LANES = 128
V7X_VMEM_LIMIT_BYTES = 56 * 1024 * 1024
TOKEN_TILE = 512
ROW_BLOCK = 64
CONV_HALO_SHORT = 8
CONV_HALO_LONG = 32
POOL_HALO = 16

_F32 = jnp.float32
_BF16 = jnp.bfloat16


def _resident(block_shape, index):
    return pl.BlockSpec(block_shape, lambda t: index, pipeline_mode=pl.Buffered(1))


def _mod_row(mod_ref, j, b):
    return mod_ref[j, pl.ds(b, 1), :]


def _modulated_rmsnorm(x, gain, scale, shift):
    ms = jnp.mean(x * x, axis=-1, keepdims=True)
    return (x * lax.rsqrt(ms + EPS)) * (gain * (1.0 + scale)) + shift


def _to_lane_blocks(ext_ref, halo, value):
    for lb in range(ext_ref.shape[0]):
        ext_ref[lb, halo:, :] = value[:, lb * V7X_LANES:(lb + 1) * V7X_LANES]


def _reset_or_keep_halo(ext_ref, halo, tile_in_seq):
    @pl.when(tile_in_seq == 0)
    def _():
        ext_ref[:, :halo, :] = jnp.zeros((ext_ref.shape[0], halo, V7X_LANES), _F32)


def _carry_halo(ext_ref, halo):
    tm = ext_ref.shape[1] - halo
    ext_ref[:, :halo, :] = ext_ref[:, tm:, :]


def _causal_dwconv(ext_ref, cw_ref, out_ref, halo):
    n_lb, rows, _ = ext_ref.shape
    taps = cw_ref.shape[0]
    first = halo - (taps - 1)

    def row_block(rb, carry):
        r0 = pl.multiple_of(rb * ROW_BLOCK, ROW_BLOCK)
        for lb in range(n_lb):
            lanes = slice(lb * V7X_LANES, (lb + 1) * V7X_LANES)
            acc = cw_ref[0:1, lanes] * ext_ref[lb, pl.ds(r0 + first, ROW_BLOCK), :]
            for k in range(1, taps):
                acc = acc + cw_ref[k:k + 1, lanes] * ext_ref[lb, pl.ds(r0 + first + k, ROW_BLOCK), :]
            out_ref[pl.ds(r0, ROW_BLOCK), lanes] = acc
        return carry

    lax.fori_loop(0, (rows - halo) // ROW_BLOCK, row_block, 0)


def _trailing_pool(ext_ref, out_ref, halo, seq_row0):
    n_lb, rows, _ = ext_ref.shape
    lb_per_group = n_lb // len(POOL_WINDOWS)

    def row_block(rb, carry):
        r0 = pl.multiple_of(rb * ROW_BLOCK, ROW_BLOCK)
        pos1 = seq_row0 + r0 + 1 + lax.broadcasted_iota(jnp.int32, (ROW_BLOCK, V7X_LANES), 0)
        for g, w in enumerate(POOL_WINDOWS):
            cnt = jnp.minimum(pos1, w).astype(_F32)
            for lb in range(g * lb_per_group, (g + 1) * lb_per_group):
                cur = ext_ref[lb, pl.ds(r0 + halo, ROW_BLOCK), :]
                acc = cur
                for j in range(1, w):
                    acc = acc + ext_ref[lb, pl.ds(r0 + halo - j, ROW_BLOCK), :]
                out_ref[pl.ds(r0, ROW_BLOCK), lb * V7X_LANES:(lb + 1) * V7X_LANES] = acc / cnt - cur
        return carry

    lax.fori_loop(0, (rows - halo) // ROW_BLOCK, row_block, 0)


def _adaln_kernel(c_ref, w_ref, b_ref, o_ref):
    c_act = jax.nn.silu(c_ref[...]).astype(_BF16)
    o_ref[...] = jnp.dot(c_act, w_ref[...].astype(_BF16), preferred_element_type=_F32) + b_ref[...]


def _adaln(c, ada_w, ada_b):
    depth, d, _ = ada_w.shape
    bsz = c.shape[0]
    return pl.pallas_call(
        _adaln_kernel,
        out_shape=jax.ShapeDtypeStruct((depth, N_MOD, bsz, d), _F32),
        grid=(depth, N_MOD),
        in_specs=[
            pl.BlockSpec((bsz, d), lambda i, j: (0, 0)),
            pl.BlockSpec((None, d, d), lambda i, j: (i, 0, j)),
            pl.BlockSpec((None, None, 1, d), lambda i, j: (i, j, 0, 0)),
        ],
        out_specs=pl.BlockSpec((None, None, bsz, d), lambda i, j: (i, j, 0, 0)),
        compiler_params=pltpu.CompilerParams(dimension_semantics=("arbitrary", "arbitrary")),
        name="adaln_mod",
    )(c, ada_w, ada_b.reshape(depth, N_MOD, 1, d))


def _mixer_a_kernel(x_ref, mod_ref, g_ref, win_ref, cw_ref, wout_ref, o_ref, pext_ref, v_ref, *, tiles_per_seq):
    t = pl.program_id(0)
    b = t // tiles_per_seq
    d = x_ref.shape[1]
    x = x_ref[...]
    h = _modulated_rmsnorm(x, g_ref[...], _mod_row(mod_ref, 1, b), _mod_row(mod_ref, 0, b))
    proj = jnp.dot(h.astype(_BF16), win_ref[...], preferred_element_type=_F32)
    gate_b = proj[:, :d]
    _reset_or_keep_halo(pext_ref, CONV_HALO_SHORT, t % tiles_per_seq)
    _to_lane_blocks(pext_ref, CONV_HALO_SHORT, proj[:, d:2 * d] * proj[:, 2 * d:])
    _causal_dwconv(pext_ref, cw_ref, v_ref, CONV_HALO_SHORT)
    _carry_halo(pext_ref, CONV_HALO_SHORT)
    y = jnp.dot((gate_b * v_ref[...]).astype(_BF16), wout_ref[...], preferred_element_type=_F32)
    o_ref[...] = x + _mod_row(mod_ref, 2, b) * y


def _mixer_b_kernel(x_ref, mod_ref, g_ref, w1_ref, b1_ref, cw_ref, cb_ref, lng_ref, lnb_ref, w2_ref, b2_ref,
                    o_ref, uext_ref, v_ref, *, tiles_per_seq):
    t = pl.program_id(0)
    b = t // tiles_per_seq
    d = x_ref.shape[1]
    x = x_ref[...]
    h = _modulated_rmsnorm(x, g_ref[...], _mod_row(mod_ref, 1, b), _mod_row(mod_ref, 0, b))
    p = jnp.dot(h.astype(_BF16), w1_ref[...], preferred_element_type=_F32) + b1_ref[...]
    _reset_or_keep_halo(uext_ref, CONV_HALO_LONG, t % tiles_per_seq)
    _to_lane_blocks(uext_ref, CONV_HALO_LONG, p[:, :d] * jax.nn.sigmoid(p[:, d:]))
    _causal_dwconv(uext_ref, cw_ref, v_ref, CONV_HALO_LONG)
    _carry_halo(uext_ref, CONV_HALO_LONG)
    v = v_ref[...] + cb_ref[...]
    mu = jnp.mean(v, axis=-1, keepdims=True)
    vc = v - mu
    var = jnp.mean(vc * vc, axis=-1, keepdims=True)
    z = jax.nn.silu(vc * lax.rsqrt(var + EPS) * lng_ref[...] + lnb_ref[...])
    y = jnp.dot(z.astype(_BF16), w2_ref[...], preferred_element_type=_F32) + b2_ref[...]
    o_ref[...] = x + _mod_row(mod_ref, 2, b) * y


def _mixer_c_kernel(x_ref, mod_ref, g_ref, wg_ref, ps_ref, o_ref, hext_ref, pooled_ref, *, tiles_per_seq):
    t = pl.program_id(0)
    b = t // tiles_per_seq
    tile_in_seq = t % tiles_per_seq
    tm = x_ref.shape[0]
    grp = wg_ref.shape[1]
    x = x_ref[...]
    h = _modulated_rmsnorm(x, g_ref[...], _mod_row(mod_ref, 1, b), _mod_row(mod_ref, 0, b))
    _reset_or_keep_halo(hext_ref, POOL_HALO, tile_in_seq)
    _to_lane_blocks(hext_ref, POOL_HALO, h)
    _trailing_pool(hext_ref, pooled_ref, POOL_HALO, tile_in_seq * tm)
    _carry_halo(hext_ref, POOL_HALO)
    pooled = pooled_ref[...].astype(_BF16)
    y = jnp.concatenate(
        [jnp.dot(pooled[:, g * grp:(g + 1) * grp], wg_ref[g], preferred_element_type=_F32)
         for g in range(wg_ref.shape[0])], axis=-1) * ps_ref[...]
    o_ref[...] = x + _mod_row(mod_ref, 2, b) * y


def _ffn_kernel(x_ref, mod_ref, g_ref, win_ref, wout_ref, fg_ref, o_ref, *, tiles_per_seq, final_norm):
    b = pl.program_id(0) // tiles_per_seq
    d_ff = wout_ref.shape[0]
    x = x_ref[...]
    h = _modulated_rmsnorm(x, g_ref[...], _mod_row(mod_ref, 4, b), _mod_row(mod_ref, 3, b))
    gu = jnp.dot(h.astype(_BF16), win_ref[...], preferred_element_type=_F32)
    a = jax.nn.silu(gu[:, :d_ff]) * gu[:, d_ff:]
    y = jnp.dot(a.astype(_BF16), wout_ref[...], preferred_element_type=_F32)
    out = x + _mod_row(mod_ref, 5, b) * y
    if final_norm:
        ms = jnp.mean(out * out, axis=-1, keepdims=True)
        out = out * lax.rsqrt(ms + EPS) * fg_ref[...]
    o_ref[...] = out


def _sublayer_call(body, name, xt, mod, layer, seq_len, params, scratch_shapes=()):
    tokens, d = xt.shape
    tm = TOKEN_TILE
    assert seq_len % tm == 0 and tm % ROW_BLOCK == 0 and d % V7X_LANES == 0
    bsz = mod.shape[2]
    x_spec = pl.BlockSpec((tm, d), lambda t: (t, 0))
    in_specs = [x_spec, _resident((None, N_MOD, bsz, d), (layer, 0, 0, 0))]
    in_specs += [_resident(shape, index) for _, shape, index in params]
    return pl.pallas_call(
        functools.partial(body, tiles_per_seq=seq_len // tm),
        out_shape=jax.ShapeDtypeStruct((tokens, d), xt.dtype),
        grid=(tokens // tm,),
        in_specs=in_specs,
        out_specs=x_spec,
        scratch_shapes=list(scratch_shapes),
        compiler_params=pltpu.CompilerParams(
            dimension_semantics=("arbitrary",), vmem_limit_bytes=V7X_VMEM_LIMIT_BYTES),
        name=name,
    )(xt, mod, *[p for p, _, _ in params])


def _row_param(stacked, idx):
    n, d = stacked.shape
    return (stacked.reshape(n, 1, d), (None, 1, d), (idx, 0, 0))


def _mat_param(stacked, idx):
    return (stacked.astype(_BF16), (None,) + stacked.shape[1:], (idx,) + (0,) * (stacked.ndim - 1))


def _ext_scratch(d, halo):
    return pltpu.VMEM((d // V7X_LANES, halo + TOKEN_TILE, V7X_LANES), _F32)


def kernel(x, c, ada_w, ada_b, norm_mix_g, norm_ffn_g, a_w_in, a_conv_w, a_w_out, b_w_pw1, b_b_pw1, b_conv_w, b_conv_b, b_ln_g, b_ln_b, b_w_pw2, b_b_pw2, p_w_grp, p_scale, ffn_w_in, ffn_w_out, final_g):
    bsz, seq_len, d = x.shape
    depth = ada_w.shape[0]
    assert p_w_grp.shape[1] == len(POOL_WINDOWS)
    mod = _adaln(c, ada_w, ada_b)
    xt = x.reshape(bsz * seq_len, d)
    tile_f32 = pltpu.VMEM((TOKEN_TILE, d), _F32)
    final_row = (final_g.reshape(1, 1, d), (None, 1, d), (0, 0, 0))
    for i in range(depth):
        kind, j = i % N_MIXERS, i // N_MIXERS
        if kind == 0:
            xt = _sublayer_call(
                _mixer_a_kernel, "mixer_short_conv", xt, mod, i, seq_len,
                [_row_param(norm_mix_g, i), _mat_param(a_w_in, j),
                 (a_conv_w, (None,) + a_conv_w.shape[1:], (j, 0, 0)), _mat_param(a_w_out, j)],
                [_ext_scratch(d, CONV_HALO_SHORT), tile_f32])
        elif kind == 1:
            xt = _sublayer_call(
                _mixer_b_kernel, "mixer_conformer", xt, mod, i, seq_len,
                [_row_param(norm_mix_g, i), _mat_param(b_w_pw1, j), _row_param(b_b_pw1, j),
                 (b_conv_w, (None,) + b_conv_w.shape[1:], (j, 0, 0)), _row_param(b_conv_b, j),
                 _row_param(b_ln_g, j), _row_param(b_ln_b, j), _mat_param(b_w_pw2, j), _row_param(b_b_pw2, j)],
                [_ext_scratch(d, CONV_HALO_LONG), tile_f32])
        else:
            xt = _sublayer_call(
                _mixer_c_kernel, "mixer_pool", xt, mod, i, seq_len,
                [_row_param(norm_mix_g, i), _mat_param(p_w_grp, j), _row_param(p_scale, j)],
                [_ext_scratch(d, POOL_HALO), tile_f32])
        xt = _sublayer_call(
            functools.partial(_ffn_kernel, final_norm=(i == depth - 1)), "ffn_swiglu", xt, mod, i, seq_len,
            [_row_param(norm_ffn_g, i), _mat_param(ffn_w_in, i), _mat_param(ffn_w_out, i), final_row])
    return xt.reshape(bsz, seq_len, d)
```

```python
import functools

import jax
import jax.numpy as jnp
from jax import lax
from jax.experimental import pallas as pl
from jax.experimental.pallas import tpu as pltpu

EPS = 1e-6
POOL_WINDOWS = (2, 4, 8, 16)
N_MIXERS = 3
N_MOD = 6

V7X_LANES = 128
V7X_VMEM_LIMIT_BYTES = 56 * 1024 * 1024
TOKEN_TILE = 512
ROW_CHAINS = 2
ROW_BLOCK = 64
CONV_HALO_SHORT = 8
CONV_HALO_LONG = 32
POOL_HALO = 16

_F32 = jnp.float32
_BF16 = jnp.bfloat16


def _resident(block_shape, index):
    return pl.BlockSpec(block_shape, lambda t: index, pipeline_mode=pl.Buffered(1))


def _row_chains(tile_rows):
    n = tile_rows // ROW_CHAINS
    return [(c * n, n) for c in range(ROW_CHAINS)]


def _mod_row(mod_ref, j, b):
    return mod_ref[j, pl.ds(b, 1), :]


def _modulated_rmsnorm(x, gain, scale, shift):
    ms = jnp.mean(x * x, axis=-1, keepdims=True)
    return (x * lax.rsqrt(ms + EPS)) * (gain * (1.0 + scale)) + shift


def _lanes(lb):
    return slice(lb * V7X_LANES, (lb + 1) * V7X_LANES)


def _zero_halo_at_sequence_start(ext_ref, halo, tile_in_seq):
    @pl.when(tile_in_seq == 0)
    def _():
        ext_ref[:, :halo, :] = jnp.zeros((ext_ref.shape[0], halo, V7X_LANES), _F32)


def _to_lane_blocks(ext_ref, halo, r0, value):
    for lb in range(ext_ref.shape[0]):
        ext_ref[lb, halo + r0:halo + r0 + value.shape[0], :] = value[:, _lanes(lb)]


def _carry_halo(ext_ref, halo):
    tm = ext_ref.shape[1] - halo
    ext_ref[:, :halo, :] = ext_ref[:, tm:, :]


def _causal_dwconv(ext_ref, cw_ref, out_ref, halo, r0, n):
    taps = cw_ref.shape[0]
    first = halo - (taps - 1)
    for r in range(r0, r0 + n, ROW_BLOCK):
        for lb in range(ext_ref.shape[0]):
            acc = cw_ref[0:1, _lanes(lb)] * ext_ref[lb, r + first:r + first + ROW_BLOCK, :]
            for k in range(1, taps):
                acc = acc + cw_ref[k:k + 1, _lanes(lb)] * ext_ref[lb, r + first + k:r + first + k + ROW_BLOCK, :]
            out_ref[r:r + ROW_BLOCK, _lanes(lb)] = acc


def _trailing_pool(ext_ref, out_ref, halo, r0, n, seq_row0):
    lb_per_group = ext_ref.shape[0] // len(POOL_WINDOWS)
    for r in range(r0, r0 + n, ROW_BLOCK):
        pos1 = seq_row0 + (r + 1) + lax.broadcasted_iota(jnp.int32, (ROW_BLOCK, V7X_LANES), 0)
        for g, w in enumerate(POOL_WINDOWS):
            cnt = jnp.minimum(pos1, w).astype(_F32)
            for lb in range(g * lb_per_group, (g + 1) * lb_per_group):
                cur = ext_ref[lb, halo + r:halo + r + ROW_BLOCK, :]
                acc = cur
                for j in range(1, w):
                    acc = acc + ext_ref[lb, halo + r - j:halo + r - j + ROW_BLOCK, :]
                out_ref[r:r + ROW_BLOCK, _lanes(lb)] = acc / cnt - cur


def _adaln_kernel(c_ref, w_ref, b_ref, o_ref):
    c_act = jax.nn.silu(c_ref[...]).astype(_BF16)
    o_ref[...] = jnp.dot(c_act, w_ref[...].astype(_BF16), preferred_element_type=_F32) + b_ref[...]


def _adaln(c, ada_w, ada_b):
    depth, d, _ = ada_w.shape
    bsz = c.shape[0]
    return pl.pallas_call(
        _adaln_kernel,
        out_shape=jax.ShapeDtypeStruct((depth, N_MOD, bsz, d), _F32),
        grid=(depth, N_MOD),
        in_specs=[
            pl.BlockSpec((bsz, d), lambda i, j: (0, 0)),
            pl.BlockSpec((None, d, d), lambda i, j: (i, 0, j)),
            pl.BlockSpec((None, None, 1, d), lambda i, j: (i, j, 0, 0)),
        ],
        out_specs=pl.BlockSpec((None, None, bsz, d), lambda i, j: (i, j, 0, 0)),
        compiler_params=pltpu.CompilerParams(dimension_semantics=("arbitrary", "arbitrary")),
        name="adaln_mod",
    )(c, ada_w, ada_b.reshape(depth, N_MOD, 1, d))


def _mixer_a_kernel(x_ref, mod_ref, g_ref, win_ref, cw_ref, wout_ref, o_ref, pext_ref, v_ref, gate_ref,
                    *, tiles_per_seq):
    t = pl.program_id(0)
    b = t // tiles_per_seq
    tm, d = x_ref.shape
    _zero_halo_at_sequence_start(pext_ref, CONV_HALO_SHORT, t % tiles_per_seq)
    shift, scale, gate = (_mod_row(mod_ref, j, b) for j in (0, 1, 2))
    for r0, n in _row_chains(tm):
        h = _modulated_rmsnorm(x_ref[r0:r0 + n, :], g_ref[...], scale, shift)
        proj = jnp.dot(h.astype(_BF16), win_ref[...], preferred_element_type=_F32)
        gate_ref[r0:r0 + n, :] = proj[:, :d]
        _to_lane_blocks(pext_ref, CONV_HALO_SHORT, r0, proj[:, d:2 * d] * proj[:, 2 * d:])
    for r0, n in _row_chains(tm):
        _causal_dwconv(pext_ref, cw_ref, v_ref, CONV_HALO_SHORT, r0, n)
        gated = gate_ref[r0:r0 + n, :] * v_ref[r0:r0 + n, :]
        y = jnp.dot(gated.astype(_BF16), wout_ref[...], preferred_element_type=_F32)
        o_ref[r0:r0 + n, :] = x_ref[r0:r0 + n, :] + gate * y
    _carry_halo(pext_ref, CONV_HALO_SHORT)


def _mixer_b_kernel(x_ref, mod_ref, g_ref, w1_ref, b1_ref, cw_ref, cb_ref, lng_ref, lnb_ref, w2_ref, b2_ref,
                    o_ref, uext_ref, v_ref, *, tiles_per_seq):
    t = pl.program_id(0)
    b = t // tiles_per_seq
    tm, d = x_ref.shape
    _zero_halo_at_sequence_start(uext_ref, CONV_HALO_LONG, t % tiles_per_seq)
    shift, scale, gate = (_mod_row(mod_ref, j, b) for j in (0, 1, 2))
    for r0, n in _row_chains(tm):
        h = _modulated_rmsnorm(x_ref[r0:r0 + n, :], g_ref[...], scale, shift)
        p = jnp.dot(h.astype(_BF16), w1_ref[...], preferred_element_type=_F32) + b1_ref[...]
        _to_lane_blocks(uext_ref, CONV_HALO_LONG, r0, p[:, :d] * jax.nn.sigmoid(p[:, d:]))
    for r0, n in _row_chains(tm):
        _causal_dwconv(uext_ref, cw_ref, v_ref, CONV_HALO_LONG, r0, n)
        v = v_ref[r0:r0 + n, :] + cb_ref[...]
        mu = jnp.mean(v, axis=-1, keepdims=True)
        vc = v - mu
        var = jnp.mean(vc * vc, axis=-1, keepdims=True)
        z = jax.nn.silu(vc * lax.rsqrt(var + EPS) * lng_ref[...] + lnb_ref[...])
        y = jnp.dot(z.astype(_BF16), w2_ref[...], preferred_element_type=_F32) + b2_ref[...]
        o_ref[r0:r0 + n, :] = x_ref[r0:r0 + n, :] + gate * y
    _carry_halo(uext_ref, CONV_HALO_LONG)


def _mixer_c_kernel(x_ref, mod_ref, g_ref, wg_ref, ps_ref, o_ref, hext_ref, pooled_ref, *, tiles_per_seq):
    t = pl.program_id(0)
    b = t // tiles_per_seq
    tile_in_seq = t % tiles_per_seq
    tm = x_ref.shape[0]
    grp = wg_ref.shape[1]
    _zero_halo_at_sequence_start(hext_ref, POOL_HALO, tile_in_seq)
    shift, scale, gate = (_mod_row(mod_ref, j, b) for j in (0, 1, 2))
    for r0, n in _row_chains(tm):
        _to_lane_blocks(hext_ref, POOL_HALO, r0, _modulated_rmsnorm(x_ref[r0:r0 + n, :], g_ref[...], scale, shift))
    for r0, n in _row_chains(tm):
        _trailing_pool(hext_ref, pooled_ref, POOL_HALO, r0, n, tile_in_seq * tm)
        pooled = pooled_ref[r0:r0 + n, :].astype(_BF16)
        y = jnp.concatenate(
            [jnp.dot(pooled[:, g * grp:(g + 1) * grp], wg_ref[g], preferred_element_type=_F32)
             for g in range(wg_ref.shape[0])], axis=-1) * ps_ref[...]
        o_ref[r0:r0 + n, :] = x_ref[r0:r0 + n, :] + gate * y
    _carry_halo(hext_ref, POOL_HALO)


def _ffn_kernel(x_ref, mod_ref, g_ref, win_ref, wout_ref, fg_ref, o_ref, *, tiles_per_seq, final_norm):
    b = pl.program_id(0) // tiles_per_seq
    d_ff = wout_ref.shape[0]
    shift, scale, gate = (_mod_row(mod_ref, j, b) for j in (3, 4, 5))
    for r0, n in _row_chains(x_ref.shape[0]):
        x = x_ref[r0:r0 + n, :]
        h = _modulated_rmsnorm(x, g_ref[...], scale, shift)
        gu = jnp.dot(h.astype(_BF16), win_ref[...], preferred_element_type=_F32)
        a = jax.nn.silu(gu[:, :d_ff]) * gu[:, d_ff:]
        y = jnp.dot(a.astype(_BF16), wout_ref[...], preferred_element_type=_F32)
        out = x + gate * y
        if final_norm:
            ms = jnp.mean(out * out, axis=-1, keepdims=True)
            out = out * lax.rsqrt(ms + EPS) * fg_ref[...]
        o_ref[r0:r0 + n, :] = out


def _sublayer_call(body, name, xt, mod, layer, seq_len, params, scratch_shapes=()):
    tokens, d = xt.shape
    tm = TOKEN_TILE
    assert seq_len % tm == 0 and tm % (ROW_CHAINS * ROW_BLOCK) == 0 and d % V7X_LANES == 0
    bsz = mod.shape[2]
    x_spec = pl.BlockSpec((tm, d), lambda t: (t, 0))
    in_specs = [x_spec, _resident((None, N_MOD, bsz, d), (layer, 0, 0, 0))]
    in_specs += [_resident(shape, index) for _, shape, index in params]
    return pl.pallas_call(
        functools.partial(body, tiles_per_seq=seq_len // tm),
        out_shape=jax.ShapeDtypeStruct((tokens, d), xt.dtype),
        grid=(tokens // tm,),
        in_specs=in_specs,
        out_specs=x_spec,
        scratch_shapes=list(scratch_shapes),
        compiler_params=pltpu.CompilerParams(
            dimension_semantics=("arbitrary",), vmem_limit_bytes=V7X_VMEM_LIMIT_BYTES),
        name=name,
    )(xt, mod, *[p for p, _, _ in params])


def _row_param(stacked, idx):
    n, d = stacked.shape
    return (stacked.reshape(n, 1, d), (None, 1, d), (idx, 0, 0))


def _mat_param(stacked, idx):
    return (stacked.astype(_BF16), (None,) + stacked.shape[1:], (idx,) + (0,) * (stacked.ndim - 1))


def _ext_scratch(d, halo):
    return pltpu.VMEM((d // V7X_LANES, halo + TOKEN_TILE, V7X_LANES), _F32)


def kernel(x, c, ada_w, ada_b, norm_mix_g, norm_ffn_g, a_w_in, a_conv_w, a_w_out, b_w_pw1, b_b_pw1, b_conv_w, b_conv_b, b_ln_g, b_ln_b, b_w_pw2, b_b_pw2, p_w_grp, p_scale, ffn_w_in, ffn_w_out, final_g):
    bsz, seq_len, d = x.shape
    depth = ada_w.shape[0]
    assert p_w_grp.shape[1] == len(POOL_WINDOWS)
    mod = _adaln(c, ada_w, ada_b)
    xt = x.reshape(bsz * seq_len, d)
    tile_f32 = pltpu.VMEM((TOKEN_TILE, d), _F32)
    final_row = (final_g.reshape(1, 1, d), (None, 1, d), (0, 0, 0))
    for i in range(depth):
        kind, j = i % N_MIXERS, i // N_MIXERS
        if kind == 0:
            xt = _sublayer_call(
                _mixer_a_kernel, "mixer_short_conv", xt, mod, i, seq_len,
                [_row_param(norm_mix_g, i), _mat_param(a_w_in, j),
                 (a_conv_w, (None,) + a_conv_w.shape[1:], (j, 0, 0)), _mat_param(a_w_out, j)],
                [_ext_scratch(d, CONV_HALO_SHORT), tile_f32, tile_f32])
        elif kind == 1:
            xt = _sublayer_call(
                _mixer_b_kernel, "mixer_conformer", xt, mod, i, seq_len,
                [_row_param(norm_mix_g, i), _mat_param(b_w_pw1, j), _row_param(b_b_pw1, j),
                 (b_conv_w, (None,) + b_conv_w.shape[1:], (j, 0, 0)), _row_param(b_conv_b, j),
                 _row_param(b_ln_g, j), _row_param(b_ln_b, j), _mat_param(b_w_pw2, j), _row_param(b_b_pw2, j)],
                [_ext_scratch(d, CONV_HALO_LONG), tile_f32])
        else:
            xt = _sublayer_call(
                _mixer_c_kernel, "mixer_pool", xt, mod, i, seq_len,
                [_row_param(norm_mix_g, i), _mat_param(p_w_grp, j), _row_param(p_scale, j)],
                [_ext_scratch(d, POOL_HALO), tile_f32])
        xt = _sublayer_call(
            functools.partial(_ffn_kernel, final_norm=(i == depth - 1)), "ffn_swiglu", xt, mod, i, seq_len,
            [_row_param(norm_ffn_g, i), _mat_param(ffn_w_in, i), _mat_param(ffn_w_out, i), final_row])
    return xt.reshape(bsz, seq_len, d)
```

```python
import functools

import jax
import jax.numpy as jnp
from jax import lax
from jax.experimental import pallas as pl
from jax.experimental.pallas import tpu as pltpu

EPS = 1e-6
POOL_WINDOWS = (2, 4, 8, 16)
N_MIXERS = 3
N_MOD = 6

V7X_LANES = 128
V7X_MXU_COLUMNS = 256
V7X_VMEM_LIMIT_BYTES = 56 * 1024 * 1024
TOKEN_TILE = 512
ROW_CHAINS = 2
ROW_BLOCK = 64
FFN_CHUNKS = 2
CONV_HALO_SHORT = 8
CONV_HALO_LONG = 32
POOL_HALO = 16

_F32 = jnp.float32
_BF16 = jnp.bfloat16


def _resident(block_shape, index):
    return pl.BlockSpec(block_shape, lambda t: index, pipeline_mode=pl.Buffered(1))


def _row_chains(tile_rows):
    n = tile_rows // ROW_CHAINS
    return [(c * n, n) for c in range(ROW_CHAINS)]


def _mod_rows(mod_ref, js, b):
    return [mod_ref[j, pl.ds(b, 1), :] for j in js]


def _modulated_rmsnorm(x, gain, scale, shift):
    ms = jnp.mean(x * x, axis=-1, keepdims=True)
    return (x * lax.rsqrt(ms + EPS)) * (gain * (1.0 + scale)) + shift


def _lanes(lb):
    return slice(lb * V7X_LANES, (lb + 1) * V7X_LANES)


def _to_lane_blocks(ext_ref, halo, r0, value):
    for lb in range(ext_ref.shape[0]):
        ext_ref[lb, halo + r0:halo + r0 + value.shape[0], :] = value[:, _lanes(lb)]


def _carry_halo(ext_ref, halo):
    tm = ext_ref.shape[1] - halo
    ext_ref[:, :halo, :] = ext_ref[:, tm:, :]


def _causal_dwconv(ext_ref, cw_ref, out_ref, halo, r0, n):
    taps = cw_ref.shape[0]
    first = halo - (taps - 1)
    for r in range(r0, r0 + n, ROW_BLOCK):
        for lb in range(ext_ref.shape[0]):
            acc = cw_ref[0:1, _lanes(lb)] * ext_ref[lb, r + first:r + first + ROW_BLOCK, :]
            for k in range(1, taps):
                acc = acc + cw_ref[k:k + 1, _lanes(lb)] * ext_ref[lb, r + first + k:r + first + k + ROW_BLOCK, :]
            out_ref[r:r + ROW_BLOCK, _lanes(lb)] = acc


def _trailing_pool(ext_ref, out_ref, halo, r0, n, seq_row0):
    lb_per_group = ext_ref.shape[0] // len(POOL_WINDOWS)
    for r in range(r0, r0 + n, ROW_BLOCK):
        pos1 = seq_row0 + (r + 1) + lax.broadcasted_iota(jnp.int32, (ROW_BLOCK, V7X_LANES), 0)
        for g, w in enumerate(POOL_WINDOWS):
            cnt = jnp.minimum(pos1, w).astype(_F32)
            for lb in range(g * lb_per_group, (g + 1) * lb_per_group):
                cur = ext_ref[lb, halo + r:halo + r + ROW_BLOCK, :]
                acc = cur
                for j in range(1, w):
                    acc = acc + ext_ref[lb, halo + r - j:halo + r - j + ROW_BLOCK, :]
                out_ref[r:r + ROW_BLOCK, _lanes(lb)] = acc / cnt - cur


def _adaln_kernel(c_ref, w_ref, b_ref, o_ref):
    c_act = jax.nn.silu(c_ref[...]).astype(_BF16)
    o_ref[...] = jnp.dot(c_act, w_ref[...].astype(_BF16), preferred_element_type=_F32) + b_ref[...]


def _adaln(c, ada_w, ada_b):
    depth, d, _ = ada_w.shape
    bsz = c.shape[0]
    return pl.pallas_call(
        _adaln_kernel,
        out_shape=jax.ShapeDtypeStruct((depth, N_MOD, bsz, d), _F32),
        grid=(depth, N_MOD),
        in_specs=[
            pl.BlockSpec((bsz, d), lambda i, j: (0, 0)),
            pl.BlockSpec((None, d, d), lambda i, j: (i, 0, j)),
            pl.BlockSpec((None, None, 1, d), lambda i, j: (i, j, 0, 0)),
        ],
        out_specs=pl.BlockSpec((None, None, bsz, d), lambda i, j: (i, j, 0, 0)),
        compiler_params=pltpu.CompilerParams(dimension_semantics=("arbitrary", "arbitrary")),
        name="adaln_mod",
    )(c, ada_w, ada_b.reshape(depth, N_MOD, 1, d))


def _short_conv_mixer(x_ref, mod, g_ref, params, scratch, seq_row0, emit):
    del seq_row0
    win_ref, cw_ref, wout_ref = params
    pext_ref, v_ref, gate_ref = scratch
    shift, scale, gate = mod
    tm, d = x_ref.shape
    for r0, n in _row_chains(tm):
        h = _modulated_rmsnorm(x_ref[r0:r0 + n, :], g_ref[...], scale, shift)
        proj = jnp.dot(h.astype(_BF16), win_ref[...], preferred_element_type=_F32)
        gate_ref[r0:r0 + n, :] = proj[:, :d]
        _to_lane_blocks(pext_ref, CONV_HALO_SHORT, r0, proj[:, d:2 * d] * proj[:, 2 * d:])
    for r0, n in _row_chains(tm):
        _causal_dwconv(pext_ref, cw_ref, v_ref, CONV_HALO_SHORT, r0, n)
        gated = gate_ref[r0:r0 + n, :] * v_ref[r0:r0 + n, :]
        y = jnp.dot(gated.astype(_BF16), wout_ref[...], preferred_element_type=_F32)
        emit(r0, n, x_ref[r0:r0 + n, :] + gate * y)
    _carry_halo(pext_ref, CONV_HALO_SHORT)


def _conformer_mixer(x_ref, mod, g_ref, params, scratch, seq_row0, emit):
    del seq_row0
    w1_ref, b1_ref, cw_ref, cb_ref, lng_ref, lnb_ref, w2_ref, b2_ref = params
    uext_ref, v_ref = scratch
    shift, scale, gate = mod
    tm, d = x_ref.shape
    for r0, n in _row_chains(tm):
        h = _modulated_rmsnorm(x_ref[r0:r0 + n, :], g_ref[...], scale, shift)
        p = jnp.dot(h.astype(_BF16), w1_ref[...], preferred_element_type=_F32) + b1_ref[...]
        _to_lane_blocks(uext_ref, CONV_HALO_LONG, r0, p[:, :d] * jax.nn.sigmoid(p[:, d:]))
    for r0, n in _row_chains(tm):
        _causal_dwconv(uext_ref, cw_ref, v_ref, CONV_HALO_LONG, r0, n)
        v = v_ref[r0:r0 + n, :] + cb_ref[...]
        mu = jnp.mean(v, axis=-1, keepdims=True)
        vc = v - mu
        var = jnp.mean(vc * vc, axis=-1, keepdims=True)
        z = jax.nn.silu(vc * lax.rsqrt(var + EPS) * lng_ref[...] + lnb_ref[...])
        y = jnp.dot(z.astype(_BF16), w2_ref[...], preferred_element_type=_F32) + b2_ref[...]
        emit(r0, n, x_ref[r0:r0 + n, :] + gate * y)
    _carry_halo(uext_ref, CONV_HALO_LONG)


def _pool_mixer(x_ref, mod, g_ref, params, scratch, seq_row0, emit):
    wg_ref, ps_ref = params
    hext_ref, pooled_ref = scratch
    shift, scale, gate = mod
    tm = x_ref.shape[0]
    grp = wg_ref.shape[1]
    for r0, n in _row_chains(tm):
        _to_lane_blocks(hext_ref, POOL_HALO, r0, _modulated_rmsnorm(x_ref[r0:r0 + n, :], g_ref[...], scale, shift))
    for r0, n in _row_chains(tm):
        _trailing_pool(hext_ref, pooled_ref, POOL_HALO, r0, n, seq_row0)
        pooled = pooled_ref[r0:r0 + n, :].astype(_BF16)
        y = jnp.concatenate(
            [jnp.dot(pooled[:, g * grp:(g + 1) * grp], wg_ref[g], preferred_element_type=_F32)
             for g in range(wg_ref.shape[0])], axis=-1) * ps_ref[...]
        emit(r0, n, x_ref[r0:r0 + n, :] + gate * y)
    _carry_halo(hext_ref, POOL_HALO)


_MIXERS = {
    0: (_short_conv_mixer, 3, CONV_HALO_SHORT),
    1: (_conformer_mixer, 8, CONV_HALO_LONG),
    2: (_pool_mixer, 2, POOL_HALO),
}


def _ffn_chunks(d_ff):
    tiles = d_ff // V7X_MXU_COLUMNS
    sizes = [(tiles // FFN_CHUNKS + (c < tiles % FFN_CHUNKS)) * V7X_MXU_COLUMNS for c in range(FFN_CHUNKS)]
    firsts = [sum(sizes[:c]) for c in range(FFN_CHUNKS)]
    return list(zip(firsts, sizes))


def _swiglu(hb, win_ref, wout_ref):
    d_ff = wout_ref.shape[0]
    y = None
    for f0, nf in _ffn_chunks(d_ff):
        g = jnp.dot(hb, win_ref[:, f0:f0 + nf], preferred_element_type=_F32)
        u = jnp.dot(hb, win_ref[:, d_ff + f0:d_ff + f0 + nf], preferred_element_type=_F32)
        part = jnp.dot((jax.nn.silu(g) * u).astype(_BF16), wout_ref[f0:f0 + nf, :], preferred_element_type=_F32)
        y = part if y is None else y + part
    return y


def _layer_kernel(*refs, kind, tiles_per_seq, final_norm):
    mixer, n_par, halo = _MIXERS[kind]
    x_ref, mod_ref, gmix_ref = refs[:3]
    mixer_params = refs[3:3 + n_par]
    gffn_ref, win_ref, wout_ref, fg_ref, o_ref = refs[3 + n_par:8 + n_par]
    mixer_scratch = refs[8 + n_par:]
    tm = x_ref.shape[0]

    t = pl.program_id(0)
    b = t // tiles_per_seq
    tile_in_seq = t % tiles_per_seq

    @pl.when(tile_in_seq == 0)
    def _():
        ext_ref = mixer_scratch[0]
        ext_ref[:, :halo, :] = jnp.zeros((ext_ref.shape[0], halo, V7X_LANES), _F32)

    shift2, scale2, gate2 = _mod_rows(mod_ref, (3, 4, 5), b)

    def ffn(r0, n, x1):
        hb = _modulated_rmsnorm(x1, gffn_ref[...], scale2, shift2).astype(_BF16)
        out = x1 + gate2 * _swiglu(hb, win_ref, wout_ref)
        if final_norm:
            ms = jnp.mean(out * out, axis=-1, keepdims=True)
            out = out * lax.rsqrt(ms + EPS) * fg_ref[...]
        o_ref[r0:r0 + n, :] = out

    mixer(x_ref, _mod_rows(mod_ref, (0, 1, 2), b), gmix_ref, mixer_params, mixer_scratch, tile_in_seq * tm, ffn)


def _layer_call(kind, name, xt, mod, layer, seq_len, mixer_params, ffn_params, mixer_scratch, final_norm):
    tokens, d = xt.shape
    tm = TOKEN_TILE
    assert seq_len % tm == 0 and tm % (ROW_CHAINS * ROW_BLOCK) == 0 and d % V7X_LANES == 0
    bsz = mod.shape[2]
    params = list(mixer_params) + list(ffn_params)
    x_spec = pl.BlockSpec((tm, d), lambda t: (t, 0))
    in_specs = [x_spec, _resident((None, N_MOD, bsz, d), (layer, 0, 0, 0))]
    in_specs += [_resident(shape, index) for _, shape, index in params]
    return pl.pallas_call(
        functools.partial(_layer_kernel, kind=kind, tiles_per_seq=seq_len // tm, final_norm=final_norm),
        out_shape=jax.ShapeDtypeStruct((tokens, d), xt.dtype),
        grid=(tokens // tm,),
        in_specs=in_specs,
        out_specs=x_spec,
        scratch_shapes=list(mixer_scratch),
        compiler_params=pltpu.CompilerParams(
            dimension_semantics=("arbitrary",), vmem_limit_bytes=V7X_VMEM_LIMIT_BYTES),
        name=name,
    )(xt, mod, *[p for p, _, _ in params])


def _row_param(stacked, idx):
    n, d = stacked.shape
    return (stacked.reshape(n, 1, d), (None, 1, d), (idx, 0, 0))


def _f32_param(stacked, idx):
    return (stacked, (None,) + stacked.shape[1:], (idx,) + (0,) * (stacked.ndim - 1))


def _mat_param(stacked, idx):
    return _f32_param(stacked.astype(_BF16), idx)


def _ext_scratch(d, halo):
    return pltpu.VMEM((d // V7X_LANES, halo + TOKEN_TILE, V7X_LANES), _F32)


def kernel(x, c, ada_w, ada_b, norm_mix_g, norm_ffn_g, a_w_in, a_conv_w, a_w_out, b_w_pw1, b_b_pw1, b_conv_w, b_conv_b, b_ln_g, b_ln_b, b_w_pw2, b_b_pw2, p_w_grp, p_scale, ffn_w_in, ffn_w_out, final_g):
    bsz, seq_len, d = x.shape
    depth = ada_w.shape[0]
    assert p_w_grp.shape[1] == len(POOL_WINDOWS)
    mod = _adaln(c, ada_w, ada_b)
    xt = x.reshape(bsz * seq_len, d)
    tile_f32 = pltpu.VMEM((TOKEN_TILE, d), _F32)
    final_row = (final_g.reshape(1, 1, d), (None, 1, d), (0, 0, 0))
    for i in range(depth):
        kind, j = i % N_MIXERS, i // N_MIXERS
        if kind == 0:
            name = "layer_short_conv"
            mixer_params = [_row_param(norm_mix_g, i), _mat_param(a_w_in, j), _f32_param(a_conv_w, j),
                            _mat_param(a_w_out, j)]
            mixer_scratch = [_ext_scratch(d, CONV_HALO_SHORT), tile_f32, tile_f32]
        elif kind == 1:
            name = "layer_conformer"
            mixer_params = [_row_param(norm_mix_g, i), _mat_param(b_w_pw1, j), _row_param(b_b_pw1, j),
                            _f32_param(b_conv_w, j), _row_param(b_conv_b, j), _row_param(b_ln_g, j),
                            _row_param(b_ln_b, j), _mat_param(b_w_pw2, j), _row_param(b_b_pw2, j)]
            mixer_scratch = [_ext_scratch(d, CONV_HALO_LONG), tile_f32]
        else:
            name = "layer_pool"
            mixer_params = [_row_param(norm_mix_g, i), _mat_param(p_w_grp, j), _row_param(p_scale, j)]
            mixer_scratch = [_ext_scratch(d, POOL_HALO), tile_f32]
        ffn_params = [_row_param(norm_ffn_g, i), _mat_param(ffn_w_in, i), _mat_param(ffn_w_out, i), final_row]
        xt = _layer_call(kind, name, xt, mod, i, seq_len, mixer_params, ffn_params, mixer_scratch,
                         final_norm=(i == depth - 1))
    return xt.reshape(bsz, seq_len, d)
```

```python
import functools

import jax
import jax.numpy as jnp
from jax import lax
from jax.experimental import pallas as pl
from jax.experimental.pallas import tpu as pltpu

EPS = 1e-6
POOL_WINDOWS = (2, 4, 8, 16)
N_MIXERS = 3
N_MOD = 6

V7X_LANES = 128
V7X_MXU_COLUMNS = 256
V7X_VMEM_LIMIT_BYTES = 56 * 1024 * 1024
ROW_CHAIN = 256
TOKEN_TILE = {0: 1024, 1: 512, 2: 1024}
ROW_BLOCK = 64
FFN_CHUNKS = 2
CONV_HALO_SHORT = 8
CONV_HALO_LONG = 32
POOL_HALO = 16

_F32 = jnp.float32
_BF16 = jnp.bfloat16


def _resident(block_shape, index):
    return pl.BlockSpec(block_shape, lambda t: index, pipeline_mode=pl.Buffered(1))


def _row_chains(tile_rows):
    return [(r0, ROW_CHAIN) for r0 in range(0, tile_rows, ROW_CHAIN)]


def _mod_rows(mod_ref, js, b):
    return [mod_ref[j, pl.ds(b, 1), :] for j in js]


def _modulated_rmsnorm(x, gain, scale, shift):
    ms = jnp.mean(x * x, axis=-1, keepdims=True)
    return (x * lax.rsqrt(ms + EPS)) * (gain * (1.0 + scale)) + shift


def _lanes(lb):
    return slice(lb * V7X_LANES, (lb + 1) * V7X_LANES)


def _to_lane_blocks(ext_ref, halo, r0, value):
    for lb in range(ext_ref.shape[0]):
        ext_ref[lb, halo + r0:halo + r0 + value.shape[0], :] = value[:, _lanes(lb)]


def _carry_halo(ext_ref, halo):
    tm = ext_ref.shape[1] - halo
    ext_ref[:, :halo, :] = ext_ref[:, tm:, :]


def _causal_dwconv(ext_ref, cw_ref, out_ref, halo, r0, n):
    taps = cw_ref.shape[0]
    first = halo - (taps - 1)
    for r in range(r0, r0 + n, ROW_BLOCK):
        for lb in range(ext_ref.shape[0]):
            acc = cw_ref[0:1, _lanes(lb)] * ext_ref[lb, r + first:r + first + ROW_BLOCK, :]
            for k in range(1, taps):
                acc = acc + cw_ref[k:k + 1, _lanes(lb)] * ext_ref[lb, r + first + k:r + first + k + ROW_BLOCK, :]
            out_ref[r:r + ROW_BLOCK, _lanes(lb)] = acc


def _trailing_pool(ext_ref, out_ref, halo, r0, n, seq_row0):
    lb_per_group = ext_ref.shape[0] // len(POOL_WINDOWS)
    for r in range(r0, r0 + n, ROW_BLOCK):
        pos1 = seq_row0 + (r + 1) + lax.broadcasted_iota(jnp.int32, (ROW_BLOCK, V7X_LANES), 0)
        for g, w in enumerate(POOL_WINDOWS):
            cnt = jnp.minimum(pos1, w).astype(_F32)
            for lb in range(g * lb_per_group, (g + 1) * lb_per_group):
                cur = ext_ref[lb, halo + r:halo + r + ROW_BLOCK, :]
                acc = cur
                for j in range(1, w):
                    acc = acc + ext_ref[lb, halo + r - j:halo + r - j + ROW_BLOCK, :]
                out_ref[r:r + ROW_BLOCK, _lanes(lb)] = acc / cnt - cur


def _adaln_kernel(c_ref, w_ref, b_ref, o_ref):
    c_act = jax.nn.silu(c_ref[...]).astype(_BF16)
    o_ref[...] = jnp.dot(c_act, w_ref[...].astype(_BF16), preferred_element_type=_F32) + b_ref[...]


def _adaln(c, ada_w, ada_b):
    depth, d, _ = ada_w.shape
    bsz = c.shape[0]
    return pl.pallas_call(
        _adaln_kernel,
        out_shape=jax.ShapeDtypeStruct((depth, N_MOD, bsz, d), _F32),
        grid=(depth, N_MOD),
        in_specs=[
            pl.BlockSpec((bsz, d), lambda i, j: (0, 0)),
            pl.BlockSpec((None, d, d), lambda i, j: (i, 0, j)),
            pl.BlockSpec((None, None, 1, d), lambda i, j: (i, j, 0, 0)),
        ],
        out_specs=pl.BlockSpec((None, None, bsz, d), lambda i, j: (i, j, 0, 0)),
        compiler_params=pltpu.CompilerParams(dimension_semantics=("arbitrary", "arbitrary")),
        name="adaln_mod",
    )(c, ada_w, ada_b.reshape(depth, N_MOD, 1, d))


def _short_conv_mixer(x_ref, mod, g_ref, params, scratch, seq_row0, emit):
    del seq_row0
    win_ref, cw_ref, wout_ref = params
    pext_ref, v_ref, gate_ref = scratch
    shift, scale, gate = mod
    tm, d = x_ref.shape
    for r0, n in _row_chains(tm):
        h = _modulated_rmsnorm(x_ref[r0:r0 + n, :], g_ref[...], scale, shift)
        proj = jnp.dot(h.astype(_BF16), win_ref[...], preferred_element_type=_F32)
        gate_ref[r0:r0 + n, :] = proj[:, :d]
        _to_lane_blocks(pext_ref, CONV_HALO_SHORT, r0, proj[:, d:2 * d] * proj[:, 2 * d:])
    for r0, n in _row_chains(tm):
        _causal_dwconv(pext_ref, cw_ref, v_ref, CONV_HALO_SHORT, r0, n)
        gated = gate_ref[r0:r0 + n, :] * v_ref[r0:r0 + n, :]
        y = jnp.dot(gated.astype(_BF16), wout_ref[...], preferred_element_type=_F32)
        emit(r0, n, x_ref[r0:r0 + n, :] + gate * y)
    _carry_halo(pext_ref, CONV_HALO_SHORT)


def _conformer_mixer(x_ref, mod, g_ref, params, scratch, seq_row0, emit):
    del seq_row0
    w1_ref, b1_ref, cw_ref, cb_ref, lng_ref, lnb_ref, w2_ref, b2_ref = params
    uext_ref, v_ref = scratch
    shift, scale, gate = mod
    tm, d = x_ref.shape
    for r0, n in _row_chains(tm):
        h = _modulated_rmsnorm(x_ref[r0:r0 + n, :], g_ref[...], scale, shift)
        p = jnp.dot(h.astype(_BF16), w1_ref[...], preferred_element_type=_F32) + b1_ref[...]
        _to_lane_blocks(uext_ref, CONV_HALO_LONG, r0, p[:, :d] * jax.nn.sigmoid(p[:, d:]))
    for r0, n in _row_chains(tm):
        _causal_dwconv(uext_ref, cw_ref, v_ref, CONV_HALO_LONG, r0, n)
        v = v_ref[r0:r0 + n, :] + cb_ref[...]
        mu = jnp.mean(v, axis=-1, keepdims=True)
        vc = v - mu
        var = jnp.mean(vc * vc, axis=-1, keepdims=True)
        z = jax.nn.silu(vc * lax.rsqrt(var + EPS) * lng_ref[...] + lnb_ref[...])
        y = jnp.dot(z.astype(_BF16), w2_ref[...], preferred_element_type=_F32) + b2_ref[...]
        emit(r0, n, x_ref[r0:r0 + n, :] + gate * y)
    _carry_halo(uext_ref, CONV_HALO_LONG)


def _pool_mixer(x_ref, mod, g_ref, params, scratch, seq_row0, emit):
    wg_ref, ps_ref = params
    hext_ref, pooled_ref = scratch
    shift, scale, gate = mod
    tm = x_ref.shape[0]
    grp = wg_ref.shape[1]
    for r0, n in _row_chains(tm):
        _to_lane_blocks(hext_ref, POOL_HALO, r0, _modulated_rmsnorm(x_ref[r0:r0 + n, :], g_ref[...], scale, shift))
    for r0, n in _row_chains(tm):
        _trailing_pool(hext_ref, pooled_ref, POOL_HALO, r0, n, seq_row0)
        pooled = pooled_ref[r0:r0 + n, :].astype(_BF16)
        y = jnp.concatenate(
            [jnp.dot(pooled[:, g * grp:(g + 1) * grp], wg_ref[g], preferred_element_type=_F32)
             for g in range(wg_ref.shape[0])], axis=-1) * ps_ref[...]
        emit(r0, n, x_ref[r0:r0 + n, :] + gate * y)
    _carry_halo(hext_ref, POOL_HALO)


_MIXERS = {
    0: (_short_conv_mixer, 3, CONV_HALO_SHORT),
    1: (_conformer_mixer, 8, CONV_HALO_LONG),
    2: (_pool_mixer, 2, POOL_HALO),
}


def _ffn_chunks(d_ff):
    tiles = d_ff // V7X_MXU_COLUMNS
    sizes = [(tiles // FFN_CHUNKS + (c < tiles % FFN_CHUNKS)) * V7X_MXU_COLUMNS for c in range(FFN_CHUNKS)]
    firsts = [sum(sizes[:c]) for c in range(FFN_CHUNKS)]
    return list(zip(firsts, sizes))


def _swiglu(hb, win_ref, wout_ref):
    d_ff = wout_ref.shape[0]
    y = None
    for f0, nf in _ffn_chunks(d_ff):
        g = jnp.dot(hb, win_ref[:, f0:f0 + nf], preferred_element_type=_F32)
        u = jnp.dot(hb, win_ref[:, d_ff + f0:d_ff + f0 + nf], preferred_element_type=_F32)
        part = jnp.dot((jax.nn.silu(g) * u).astype(_BF16), wout_ref[f0:f0 + nf, :], preferred_element_type=_F32)
        y = part if y is None else y + part
    return y


def _layer_kernel(*refs, kind, tiles_per_seq, final_norm):
    mixer, n_par, halo = _MIXERS[kind]
    x_ref, mod_ref, gmix_ref = refs[:3]
    mixer_params = refs[3:3 + n_par]
    gffn_ref, win_ref, wout_ref, fg_ref, o_ref = refs[3 + n_par:8 + n_par]
    mixer_scratch = refs[8 + n_par:]
    tm = x_ref.shape[0]

    t = pl.program_id(0)
    b = t // tiles_per_seq
    tile_in_seq = t % tiles_per_seq

    @pl.when(tile_in_seq == 0)
    def _():
        ext_ref = mixer_scratch[0]
        ext_ref[:, :halo, :] = jnp.zeros((ext_ref.shape[0], halo, V7X_LANES), _F32)

    shift2, scale2, gate2 = _mod_rows(mod_ref, (3, 4, 5), b)

    def ffn(r0, n, x1):
        hb = _modulated_rmsnorm(x1, gffn_ref[...], scale2, shift2).astype(_BF16)
        out = x1 + gate2 * _swiglu(hb, win_ref, wout_ref)
        if final_norm:
            ms = jnp.mean(out * out, axis=-1, keepdims=True)
            out = out * lax.rsqrt(ms + EPS) * fg_ref[...]
        o_ref[r0:r0 + n, :] = out

    mixer(x_ref, _mod_rows(mod_ref, (0, 1, 2), b), gmix_ref, mixer_params, mixer_scratch, tile_in_seq * tm, ffn)


def _layer_call(kind, name, xt, mod, layer, seq_len, mixer_params, ffn_params, mixer_scratch, final_norm):
    tokens, d = xt.shape
    tm = TOKEN_TILE[kind]
    assert seq_len % tm == 0 and tm % ROW_CHAIN == 0 and ROW_CHAIN % ROW_BLOCK == 0 and d % V7X_LANES == 0
    bsz = mod.shape[2]
    params = list(mixer_params) + list(ffn_params)
    x_spec = pl.BlockSpec((tm, d), lambda t: (t, 0))
    in_specs = [x_spec, _resident((None, N_MOD, bsz, d), (layer, 0, 0, 0))]
    in_specs += [_resident(shape, index) for _, shape, index in params]
    return pl.pallas_call(
        functools.partial(_layer_kernel, kind=kind, tiles_per_seq=seq_len // tm, final_norm=final_norm),
        out_shape=jax.ShapeDtypeStruct((tokens, d), xt.dtype),
        grid=(tokens // tm,),
        in_specs=in_specs,
        out_specs=x_spec,
        scratch_shapes=list(mixer_scratch),
        compiler_params=pltpu.CompilerParams(
            dimension_semantics=("arbitrary",), vmem_limit_bytes=V7X_VMEM_LIMIT_BYTES),
        name=name,
    )(xt, mod, *[p for p, _, _ in params])


def _row_param(stacked, idx):
    n, d = stacked.shape
    return (stacked.reshape(n, 1, d), (None, 1, d), (idx, 0, 0))


def _f32_param(stacked, idx):
    return (stacked, (None,) + stacked.shape[1:], (idx,) + (0,) * (stacked.ndim - 1))


def _mat_param(stacked, idx):
    return _f32_param(stacked.astype(_BF16), idx)


def _ext_scratch(tm, d, halo):
    return pltpu.VMEM((d // V7X_LANES, halo + tm, V7X_LANES), _F32)


def kernel(x, c, ada_w, ada_b, norm_mix_g, norm_ffn_g, a_w_in, a_conv_w, a_w_out, b_w_pw1, b_b_pw1, b_conv_w, b_conv_b, b_ln_g, b_ln_b, b_w_pw2, b_b_pw2, p_w_grp, p_scale, ffn_w_in, ffn_w_out, final_g):
    bsz, seq_len, d = x.shape
    depth = ada_w.shape[0]
    assert p_w_grp.shape[1] == len(POOL_WINDOWS)
    mod = _adaln(c, ada_w, ada_b)
    xt = x.reshape(bsz * seq_len, d)
    final_row = (final_g.reshape(1, 1, d), (None, 1, d), (0, 0, 0))
    for i in range(depth):
        kind, j = i % N_MIXERS, i // N_MIXERS
        tm = TOKEN_TILE[kind]
        tile_f32 = pltpu.VMEM((tm, d), _F32)
        if kind == 0:
            name = "layer_short_conv"
            mixer_params = [_row_param(norm_mix_g, i), _mat_param(a_w_in, j), _f32_param(a_conv_w, j),
                            _mat_param(a_w_out, j)]
            mixer_scratch = [_ext_scratch(tm, d, CONV_HALO_SHORT), tile_f32, tile_f32]
        elif kind == 1:
            name = "layer_conformer"
            mixer_params = [_row_param(norm_mix_g, i), _mat_param(b_w_pw1, j), _row_param(b_b_pw1, j),
                            _f32_param(b_conv_w, j), _row_param(b_conv_b, j), _row_param(b_ln_g, j),
                            _row_param(b_ln_b, j), _mat_param(b_w_pw2, j), _row_param(b_b_pw2, j)]
            mixer_scratch = [_ext_scratch(tm, d, CONV_HALO_LONG), tile_f32]
        else:
            name = "layer_pool"
            mixer_params = [_row_param(norm_mix_g, i), _mat_param(p_w_grp, j), _row_param(p_scale, j)]
            mixer_scratch = [_ext_scratch(tm, d, POOL_HALO), tile_f32]
        ffn_params = [_row_param(norm_ffn_g, i), _mat_param(ffn_w_in, i), _mat_param(ffn_w_out, i), final_row]
        xt = _layer_call(kind, name, xt, mod, i, seq_len, mixer_params, ffn_params, mixer_scratch,
                         final_norm=(i == depth - 1))
    return xt.reshape(bsz, seq_len, d)
```

```python
import functools
import math

import jax
import jax.numpy as jnp
from jax import lax
from jax.experimental import pallas as pl
from jax.experimental.pallas import tpu as pltpu

EPS = 1e-6
POOL_WINDOWS = (2, 4, 8, 16)
N_MIXERS = 3
N_MOD = 6

V7X_LANES = 128
V7X_BF16_SUBLANES = 16
V7X_MXU_COLUMNS = 256
V7X_VMEM_LIMIT_BYTES = 56 * 1024 * 1024
ROW_CHAIN = 256
TOKEN_TILE = {0: 512, 1: 512, 2: 1024}
ROW_BLOCK = 64
FFN_CHUNKS = 2
CONV_HALO_SHORT = 8
CONV_HALO_LONG = 32
POOL_HALO = 16

_F32 = jnp.float32
_BF16 = jnp.bfloat16


def _resident(block_shape, index):
    return pl.BlockSpec(block_shape, lambda t: index, pipeline_mode=pl.Buffered(1))


def _row_chains(tile_rows):
    return [(r0, ROW_CHAIN) for r0 in range(0, tile_rows, ROW_CHAIN)]


def _mod_rows(mod_ref, js, b):
    return [mod_ref[j, pl.ds(b, 1), :] for j in js]


def _modulated_rmsnorm(x, gain, scale, shift):
    ms = jnp.mean(x * x, axis=-1, keepdims=True)
    return (x * lax.rsqrt(ms + EPS)) * (gain * (1.0 + scale)) + shift


def _lanes(lb):
    return slice(lb * V7X_LANES, (lb + 1) * V7X_LANES)


def _to_lane_blocks(ext_ref, halo, r0, value):
    for lb in range(ext_ref.shape[0]):
        ext_ref[lb, halo + r0:halo + r0 + value.shape[0], :] = value[:, _lanes(lb)]


def _carry_halo(ext_ref, halo):
    tm = ext_ref.shape[1] - halo
    ext_ref[:, :halo, :] = ext_ref[:, tm:, :]


def _causal_dwconv(ext_ref, cw_ref, out_ref, halo, r0, n):
    taps = cw_ref.shape[0]
    first = halo - (taps - 1)
    for r in range(r0, r0 + n, ROW_BLOCK):
        for lb in range(ext_ref.shape[0]):
            acc = cw_ref[0:1, _lanes(lb)] * ext_ref[lb, r + first:r + first + ROW_BLOCK, :]
            for k in range(1, taps):
                acc = acc + cw_ref[k:k + 1, _lanes(lb)] * ext_ref[lb, r + first + k:r + first + k + ROW_BLOCK, :]
            out_ref[r:r + ROW_BLOCK, _lanes(lb)] = acc


def _trailing_pool(ext_ref, out_ref, halo, r0, n, seq_row0):
    lb_per_group = ext_ref.shape[0] // len(POOL_WINDOWS)
    for r in range(r0, r0 + n, ROW_BLOCK):
        pos1 = seq_row0 + (r + 1) + lax.broadcasted_iota(jnp.int32, (ROW_BLOCK, V7X_LANES), 0)
        for g, w in enumerate(POOL_WINDOWS):
            cnt = jnp.minimum(pos1, w).astype(_F32)
            for lb in range(g * lb_per_group, (g + 1) * lb_per_group):
                cur = ext_ref[lb, halo + r:halo + r + ROW_BLOCK, :]
                acc = cur
                for j in range(1, w):
                    acc = acc + ext_ref[lb, halo + r - j:halo + r - j + ROW_BLOCK, :]
                out_ref[r:r + ROW_BLOCK, _lanes(lb)] = acc / cnt - cur


def _adaln_kernel(c_ref, w_ref, b_ref, o_ref):
    c_act = jax.nn.silu(c_ref[...]).astype(_BF16)
    o_ref[...] = jnp.dot(c_act, w_ref[...].astype(_BF16), preferred_element_type=_F32) + b_ref[...]


def _adaln(c, ada_w, ada_b):
    depth, d, _ = ada_w.shape
    bsz = c.shape[0]
    return pl.pallas_call(
        _adaln_kernel,
        out_shape=jax.ShapeDtypeStruct((depth, N_MOD, bsz, d), _F32),
        grid=(depth, N_MOD),
        in_specs=[
            pl.BlockSpec((bsz, d), lambda i, j: (0, 0)),
            pl.BlockSpec((None, d, d), lambda i, j: (i, 0, j)),
            pl.BlockSpec((None, None, 1, d), lambda i, j: (i, j, 0, 0)),
        ],
        out_specs=pl.BlockSpec((None, None, bsz, d), lambda i, j: (i, j, 0, 0)),
        compiler_params=pltpu.CompilerParams(dimension_semantics=("arbitrary", "arbitrary")),
        name="adaln_mod",
    )(c, ada_w, ada_b.reshape(depth, N_MOD, 1, d))


def _short_conv_mixer(x_ref, mod, g_ref, params, scratch, seq_row0, emit):
    del seq_row0
    win_ref, cw_ref, wout_ref = params
    pext_ref, v_ref, gate_ref = scratch
    shift, scale, gate = mod
    tm, d = x_ref.shape
    for r0, n in _row_chains(tm):
        h = _modulated_rmsnorm(x_ref[r0:r0 + n, :], g_ref[...], scale, shift)
        proj = jnp.dot(h.astype(_BF16), win_ref[...], preferred_element_type=_F32)
        gate_ref[r0:r0 + n, :] = proj[:, :d]
        _to_lane_blocks(pext_ref, CONV_HALO_SHORT, r0, proj[:, d:2 * d] * proj[:, 2 * d:])
    for r0, n in _row_chains(tm):
        _causal_dwconv(pext_ref, cw_ref, v_ref, CONV_HALO_SHORT, r0, n)
        gated = gate_ref[r0:r0 + n, :] * v_ref[r0:r0 + n, :]
        y = jnp.dot(gated.astype(_BF16), wout_ref[...], preferred_element_type=_F32)
        emit(r0, n, x_ref[r0:r0 + n, :] + gate * y)
    _carry_halo(pext_ref, CONV_HALO_SHORT)


def _conformer_mixer(x_ref, mod, g_ref, params, scratch, seq_row0, emit):
    del seq_row0
    w1_ref, b1_ref, cw_ref, cb_ref, lng_ref, lnb_ref, w2_ref, b2_ref = params
    uext_ref, v_ref = scratch
    shift, scale, gate = mod
    tm, d = x_ref.shape
    for r0, n in _row_chains(tm):
        h = _modulated_rmsnorm(x_ref[r0:r0 + n, :], g_ref[...], scale, shift)
        p = jnp.dot(h.astype(_BF16), w1_ref[...], preferred_element_type=_F32) + b1_ref[...]
        _to_lane_blocks(uext_ref, CONV_HALO_LONG, r0, p[:, :d] * jax.nn.sigmoid(p[:, d:]))
    for r0, n in _row_chains(tm):
        _causal_dwconv(uext_ref, cw_ref, v_ref, CONV_HALO_LONG, r0, n)
        v = v_ref[r0:r0 + n, :] + cb_ref[...]
        mu = jnp.mean(v, axis=-1, keepdims=True)
        vc = v - mu
        var = jnp.mean(vc * vc, axis=-1, keepdims=True)
        z = jax.nn.silu(vc * lax.rsqrt(var + EPS) * lng_ref[...] + lnb_ref[...])
        y = jnp.dot(z.astype(_BF16), w2_ref[...], preferred_element_type=_F32) + b2_ref[...]
        emit(r0, n, x_ref[r0:r0 + n, :] + gate * y)
    _carry_halo(uext_ref, CONV_HALO_LONG)


def _pool_mixer(x_ref, mod, g_ref, params, scratch, seq_row0, emit):
    wg_ref, ps_ref = params
    hext_ref, pooled_ref = scratch
    shift, scale, gate = mod
    tm = x_ref.shape[0]
    grp = wg_ref.shape[1]
    for r0, n in _row_chains(tm):
        _to_lane_blocks(hext_ref, POOL_HALO, r0, _modulated_rmsnorm(x_ref[r0:r0 + n, :], g_ref[...], scale, shift))
    for r0, n in _row_chains(tm):
        _trailing_pool(hext_ref, pooled_ref, POOL_HALO, r0, n, seq_row0)
        pooled = pooled_ref[r0:r0 + n, :].astype(_BF16)
        y = jnp.concatenate(
            [jnp.dot(pooled[:, g * grp:(g + 1) * grp], wg_ref[g], preferred_element_type=_F32)
             for g in range(wg_ref.shape[0])], axis=-1) * ps_ref[...]
        emit(r0, n, x_ref[r0:r0 + n, :] + gate * y)
    _carry_halo(hext_ref, POOL_HALO)


_MIXERS = {
    0: (_short_conv_mixer, 3, CONV_HALO_SHORT),
    1: (_conformer_mixer, 8, CONV_HALO_LONG),
    2: (_pool_mixer, 2, POOL_HALO),
}


def _ffn_chunks(d_ff):
    tiles = d_ff // V7X_MXU_COLUMNS
    sizes = [(tiles // FFN_CHUNKS + (c < tiles % FFN_CHUNKS)) * V7X_MXU_COLUMNS for c in range(FFN_CHUNKS)]
    firsts = [sum(sizes[:c]) for c in range(FFN_CHUNKS)]
    return list(zip(firsts, sizes))


def _swiglu(hb, win_ref, wout_ref):
    d_ff = wout_ref.shape[0]
    y = None
    for f0, nf in _ffn_chunks(d_ff):
        g = jnp.dot(hb, win_ref[:, f0:f0 + nf], preferred_element_type=_F32)
        u = jnp.dot(hb, win_ref[:, d_ff + f0:d_ff + f0 + nf], preferred_element_type=_F32)
        part = jnp.dot((jax.nn.silu(g) * u).astype(_BF16), wout_ref[f0:f0 + nf, :], preferred_element_type=_F32)
        y = part if y is None else y + part
    return y


def _layer_kernel(*refs, kind, tiles_per_seq, final_norm, n_cast):
    mixer, n_par, halo = _MIXERS[kind]
    x_ref, mod_ref, gmix_ref = refs[:3]
    mixer_params = refs[3:3 + n_par]
    gffn_ref, win_ref, wout_ref, fg_ref = refs[3 + n_par:7 + n_par]
    cast_in = refs[7 + n_par:7 + n_par + n_cast]
    o_ref = refs[7 + n_par + n_cast]
    cast_out = refs[8 + n_par + n_cast:8 + n_par + 2 * n_cast]
    mixer_scratch = refs[8 + n_par + 2 * n_cast:]
    tm = x_ref.shape[0]

    for w_f32, w_bf16 in zip(cast_in, cast_out):
        w_bf16[...] = w_f32[...].astype(_BF16)

    t = pl.program_id(0)
    b = t // tiles_per_seq
    tile_in_seq = t % tiles_per_seq

    @pl.when(tile_in_seq == 0)
    def _():
        ext_ref = mixer_scratch[0]
        ext_ref[:, :halo, :] = jnp.zeros((ext_ref.shape[0], halo, V7X_LANES), _F32)

    shift2, scale2, gate2 = _mod_rows(mod_ref, (3, 4, 5), b)

    def ffn(r0, n, x1):
        hb = _modulated_rmsnorm(x1, gffn_ref[...], scale2, shift2).astype(_BF16)
        out = x1 + gate2 * _swiglu(hb, win_ref, wout_ref)
        if final_norm:
            ms = jnp.mean(out * out, axis=-1, keepdims=True)
            out = out * lax.rsqrt(ms + EPS) * fg_ref[...]
        o_ref[r0:r0 + n, :] = out

    mixer(x_ref, _mod_rows(mod_ref, (0, 1, 2), b), gmix_ref, mixer_params, mixer_scratch, tile_in_seq * tm, ffn)


def _layer_call(kind, name, xt, mod, layer, seq_len, mixer_params, ffn_params, mixer_scratch, final_norm,
                cast_jobs=()):
    tokens, d = xt.shape
    tm = TOKEN_TILE[kind]
    assert seq_len % tm == 0 and tm % ROW_CHAIN == 0 and ROW_CHAIN % ROW_BLOCK == 0 and d % V7X_LANES == 0
    steps = tokens // tm
    bsz = mod.shape[2]
    params = list(mixer_params) + list(ffn_params)
    x_spec = pl.BlockSpec((tm, d), lambda t: (t, 0))
    in_specs = [x_spec, _resident((None, N_MOD, bsz, d), (layer, 0, 0, 0))]
    in_specs += [_resident(shape, index) for _, shape, index in params]
    view_rows = steps * V7X_BF16_SUBLANES
    cast_views, cast_shapes, cast_out_specs = [], [], []
    for stack, idx in cast_jobs:
        cols = math.prod(stack.shape[1:]) // view_rows
        assert cols * view_rows == math.prod(stack.shape[1:]) and cols % V7X_LANES == 0
        cast_views.append(stack.reshape(stack.shape[0] * view_rows, cols))
        in_specs.append(pl.BlockSpec((V7X_BF16_SUBLANES, cols), lambda t, first=idx * steps: (first + t, 0)))
        cast_out_specs.append(pl.BlockSpec((V7X_BF16_SUBLANES, cols), lambda t: (t, 0)))
        cast_shapes.append(jax.ShapeDtypeStruct((view_rows, cols), _BF16))
    out, *casted = pl.pallas_call(
        functools.partial(_layer_kernel, kind=kind, tiles_per_seq=seq_len // tm, final_norm=final_norm,
                          n_cast=len(cast_jobs)),
        out_shape=[jax.ShapeDtypeStruct((tokens, d), xt.dtype)] + cast_shapes,
        grid=(steps,),
        in_specs=in_specs,
        out_specs=[x_spec] + cast_out_specs,
        scratch_shapes=list(mixer_scratch),
        compiler_params=pltpu.CompilerParams(
            dimension_semantics=("arbitrary",), vmem_limit_bytes=V7X_VMEM_LIMIT_BYTES),
        name=name,
    )(xt, mod, *[p for p, _, _ in params], *cast_views)
    return out, [w.reshape(stack.shape[1:]) for w, (stack, _) in zip(casted, cast_jobs)]


def _row_param(stacked, idx):
    n, d = stacked.shape
    return (stacked.reshape(n, 1, d), (None, 1, d), (idx, 0, 0))


def _stack_param(stacked, idx):
    return (stacked, (None,) + stacked.shape[1:], (idx,) + (0,) * (stacked.ndim - 1))


def _whole_param(array):
    return (array, array.shape, (0,) * array.ndim)


def _ext_scratch(tm, d, halo):
    return pltpu.VMEM((d // V7X_LANES, halo + tm, V7X_LANES), _F32)


def kernel(x, c, ada_w, ada_b, norm_mix_g, norm_ffn_g, a_w_in, a_conv_w, a_w_out, b_w_pw1, b_b_pw1, b_conv_w, b_conv_b, b_ln_g, b_ln_b, b_w_pw2, b_b_pw2, p_w_grp, p_scale, ffn_w_in, ffn_w_out, final_g):
    bsz, seq_len, d = x.shape
    depth = ada_w.shape[0]
    assert p_w_grp.shape[1] == len(POOL_WINDOWS)
    mod = _adaln(c, ada_w, ada_b)
    xt = x.reshape(bsz * seq_len, d)
    final_row = (final_g.reshape(1, 1, d), (None, 1, d), (0, 0, 0))

    def mxu_weights(i):
        kind, j = i % N_MIXERS, i // N_MIXERS
        mixer = ([(a_w_in, j), (a_w_out, j)], [(b_w_pw1, j), (b_w_pw2, j)], [(p_w_grp, j)])[kind]
        return mixer + [(ffn_w_in, i), (ffn_w_out, i)]

    later = [job for i in range(1, depth) for job in mxu_weights(i)]
    bf16_weights = {0: [stack[idx].astype(_BF16) for stack, idx in mxu_weights(0)]}
    for i in range(depth):
        kind, j = i % N_MIXERS, i // N_MIXERS
        tm = TOKEN_TILE[kind]
        tile_f32 = pltpu.VMEM((tm, d), _F32)
        *mixer_w, w_in, w_out = [_whole_param(w) for w in bf16_weights[i]]
        if kind == 0:
            name = "layer_short_conv"
            mixer_params = [_row_param(norm_mix_g, i), mixer_w[0], _stack_param(a_conv_w, j), mixer_w[1]]
            mixer_scratch = [_ext_scratch(tm, d, CONV_HALO_SHORT), tile_f32, tile_f32]
        elif kind == 1:
            name = "layer_conformer"
            mixer_params = [_row_param(norm_mix_g, i), mixer_w[0], _row_param(b_b_pw1, j),
                            _stack_param(b_conv_w, j), _row_param(b_conv_b, j), _row_param(b_ln_g, j),
                            _row_param(b_ln_b, j), mixer_w[1], _row_param(b_b_pw2, j)]
            mixer_scratch = [_ext_scratch(tm, d, CONV_HALO_LONG), tile_f32]
        else:
            name = "layer_pool"
            mixer_params = [_row_param(norm_mix_g, i), mixer_w[0], _row_param(p_scale, j)]
            mixer_scratch = [_ext_scratch(tm, d, POOL_HALO), tile_f32]
        ffn_params = [_row_param(norm_ffn_g, i), w_in, w_out, final_row]
        xt, casted = _layer_call(kind, name, xt, mod, i, seq_len, mixer_params, ffn_params, mixer_scratch,
                                 final_norm=(i == depth - 1), cast_jobs=later if i == 0 else ())
        if i == 0:
            for li in range(1, depth):
                n = len(mxu_weights(li))
                bf16_weights[li], casted = casted[:n], casted[n:]
    return xt.reshape(bsz, seq_len, d)
```

```python
import functools
import math

import jax
import jax.numpy as jnp
from jax import lax
from jax.experimental import pallas as pl
from jax.experimental.pallas import tpu as pltpu

EPS = 1e-6
POOL_WINDOWS = (2, 4, 8, 16)
N_MIXERS = 3
N_MOD = 6

V7X_LANES = 128
V7X_BF16_SUBLANES = 16
V7X_MXU_COLUMNS = 256
V7X_VMEM_LIMIT_BYTES = 56 * 1024 * 1024
ROW_CHAIN = 256
TOKEN_TILE = {0: 512, 1: 512, 2: 1024}
ROW_BLOCK = 64
FFN_CHUNKS = 2
CONV_HALO_SHORT = 8
CONV_HALO_LONG = 32
POOL_HALO = 16

_F32 = jnp.float32
_BF16 = jnp.bfloat16


def _resident(block_shape, index):
    return pl.BlockSpec(block_shape, lambda t: index, pipeline_mode=pl.Buffered(1))


def _row_chains(tile_rows):
    return [(r0, ROW_CHAIN) for r0 in range(0, tile_rows, ROW_CHAIN)]


def _mod_rows(mod_ref, js, b):
    return [mod_ref[j, pl.ds(b, 1), :] for j in js]


def _modulated_rmsnorm(x, gain, scale, shift):
    ms = jnp.mean(x * x, axis=-1, keepdims=True)
    return (x * lax.rsqrt(ms + EPS)) * (gain * (1.0 + scale)) + shift


def _lanes(lb):
    return slice(lb * V7X_LANES, (lb + 1) * V7X_LANES)


def _to_lane_blocks(ext_ref, halo, r0, value):
    for lb in range(ext_ref.shape[0]):
        ext_ref[lb, halo + r0:halo + r0 + value.shape[0], :] = value[:, _lanes(lb)]


def _carry_halo(ext_ref, halo):
    tm = ext_ref.shape[1] - halo
    ext_ref[:, :halo, :] = ext_ref[:, tm:, :]


def _causal_dwconv(ext_ref, cw_ref, out_ref, halo, r0, n):
    taps = cw_ref.shape[0]
    first = halo - (taps - 1)
    for r in range(r0, r0 + n, ROW_BLOCK):
        for lb in range(ext_ref.shape[0]):
            acc = cw_ref[0:1, _lanes(lb)] * ext_ref[lb, r + first:r + first + ROW_BLOCK, :]
            for k in range(1, taps):
                acc = acc + cw_ref[k:k + 1, _lanes(lb)] * ext_ref[lb, r + first + k:r + first + k + ROW_BLOCK, :]
            out_ref[r:r + ROW_BLOCK, _lanes(lb)] = acc


def _trailing_pool(ext_ref, out_ref, halo, r0, n, seq_row0):
    lb_per_group = ext_ref.shape[0] // len(POOL_WINDOWS)
    for r in range(r0, r0 + n, ROW_BLOCK):
        pos1 = seq_row0 + (r + 1) + lax.broadcasted_iota(jnp.int32, (ROW_BLOCK, V7X_LANES), 0)
        for g, w in enumerate(POOL_WINDOWS):
            cnt = jnp.minimum(pos1, w).astype(_F32)
            for lb in range(g * lb_per_group, (g + 1) * lb_per_group):
                cur = ext_ref[lb, halo + r:halo + r + ROW_BLOCK, :]
                acc = cur
                for j in range(1, w):
                    acc = acc + ext_ref[lb, halo + r - j:halo + r - j + ROW_BLOCK, :]
                out_ref[r:r + ROW_BLOCK, _lanes(lb)] = acc / cnt - cur


def _adaln_kernel(c_ref, w_ref, b_ref, o_ref):
    c_act = jax.nn.silu(c_ref[...]).astype(_BF16)
    o_ref[...] = jnp.dot(c_act, w_ref[...].astype(_BF16), preferred_element_type=_F32) + b_ref[...]


def _adaln(c, ada_w, ada_b):
    depth, d, _ = ada_w.shape
    bsz = c.shape[0]
    return pl.pallas_call(
        _adaln_kernel,
        out_shape=jax.ShapeDtypeStruct((depth, N_MOD, bsz, d), _F32),
        grid=(depth, N_MOD),
        in_specs=[
            pl.BlockSpec((bsz, d), lambda i, j: (0, 0)),
            pl.BlockSpec((None, d, d), lambda i, j: (i, 0, j)),
            pl.BlockSpec((None, None, 1, d), lambda i, j: (i, j, 0, 0)),
        ],
        out_specs=pl.BlockSpec((None, None, bsz, d), lambda i, j: (i, j, 0, 0)),
        compiler_params=pltpu.CompilerParams(dimension_semantics=("arbitrary", "arbitrary")),
        name="adaln_mod",
    )(c, ada_w, ada_b.reshape(depth, N_MOD, 1, d))


def _short_conv_mixer(x_ref, mod, g_ref, params, scratch, seq_row0, emit):
    del seq_row0
    win_ref, cw_ref, wout_ref = params
    pext_ref, v_ref, gate_ref = scratch
    shift, scale, gate = mod
    tm, d = x_ref.shape
    for r0, n in _row_chains(tm):
        h = _modulated_rmsnorm(x_ref[r0:r0 + n, :], g_ref[...], scale, shift)
        proj = jnp.dot(h.astype(_BF16), win_ref[...], preferred_element_type=_F32)
        gate_ref[r0:r0 + n, :] = proj[:, :d]
        _to_lane_blocks(pext_ref, CONV_HALO_SHORT, r0, proj[:, d:2 * d] * proj[:, 2 * d:])
    for r0, n in _row_chains(tm):
        _causal_dwconv(pext_ref, cw_ref, v_ref, CONV_HALO_SHORT, r0, n)
        gated = gate_ref[r0:r0 + n, :] * v_ref[r0:r0 + n, :]
        y = jnp.dot(gated.astype(_BF16), wout_ref[...], preferred_element_type=_F32)
        emit(r0, n, x_ref[r0:r0 + n, :] + gate * y)
    _carry_halo(pext_ref, CONV_HALO_SHORT)


def _conformer_mixer(x_ref, mod, g_ref, params, scratch, seq_row0, emit):
    del seq_row0
    w1_ref, b1_ref, cw_ref, cb_ref, lng_ref, lnb_ref, w2_ref, b2_ref = params
    uext_ref, v_ref = scratch
    shift, scale, gate = mod
    tm, d = x_ref.shape
    for r0, n in _row_chains(tm):
        h = _modulated_rmsnorm(x_ref[r0:r0 + n, :], g_ref[...], scale, shift)
        p = jnp.dot(h.astype(_BF16), w1_ref[...], preferred_element_type=_F32) + b1_ref[...]
        _to_lane_blocks(uext_ref, CONV_HALO_LONG, r0, p[:, :d] * jax.nn.sigmoid(p[:, d:]))
    for r0, n in _row_chains(tm):
        _causal_dwconv(uext_ref, cw_ref, v_ref, CONV_HALO_LONG, r0, n)
        v = v_ref[r0:r0 + n, :] + cb_ref[...]
        mu = jnp.mean(v, axis=-1, keepdims=True)
        vc = v - mu
        var = jnp.mean(vc * vc, axis=-1, keepdims=True)
        z = jax.nn.silu(vc * lax.rsqrt(var + EPS) * lng_ref[...] + lnb_ref[...])
        y = jnp.dot(z.astype(_BF16), w2_ref[...], preferred_element_type=_F32) + b2_ref[...]
        emit(r0, n, x_ref[r0:r0 + n, :] + gate * y)
    _carry_halo(uext_ref, CONV_HALO_LONG)


def _pool_mixer(x_ref, mod, g_ref, params, scratch, seq_row0, emit):
    wg_ref, ps_ref = params
    hext_ref, pooled_ref = scratch
    shift, scale, gate = mod
    tm = x_ref.shape[0]
    grp = wg_ref.shape[1]
    for r0, n in _row_chains(tm):
        _to_lane_blocks(hext_ref, POOL_HALO, r0, _modulated_rmsnorm(x_ref[r0:r0 + n, :], g_ref[...], scale, shift))
    for r0, n in _row_chains(tm):
        _trailing_pool(hext_ref, pooled_ref, POOL_HALO, r0, n, seq_row0)
        pooled = pooled_ref[r0:r0 + n, :].astype(_BF16)
        y = jnp.concatenate(
            [jnp.dot(pooled[:, g * grp:(g + 1) * grp], wg_ref[g], preferred_element_type=_F32)
             for g in range(wg_ref.shape[0])], axis=-1) * ps_ref[...]
        emit(r0, n, x_ref[r0:r0 + n, :] + gate * y)
    _carry_halo(hext_ref, POOL_HALO)


_MIXERS = {
    0: (_short_conv_mixer, 3, CONV_HALO_SHORT),
    1: (_conformer_mixer, 8, CONV_HALO_LONG),
    2: (_pool_mixer, 2, POOL_HALO),
}


def _ffn_chunks(d_ff):
    tiles = d_ff // V7X_MXU_COLUMNS
    sizes = [(tiles // FFN_CHUNKS + (c < tiles % FFN_CHUNKS)) * V7X_MXU_COLUMNS for c in range(FFN_CHUNKS)]
    firsts = [sum(sizes[:c]) for c in range(FFN_CHUNKS)]
    return list(zip(firsts, sizes))


def _swiglu(hb, win_ref, wout_ref):
    d_ff = wout_ref.shape[0]
    y = None
    for f0, nf in _ffn_chunks(d_ff):
        g = jnp.dot(hb, win_ref[:, f0:f0 + nf], preferred_element_type=_F32)
        u = jnp.dot(hb, win_ref[:, d_ff + f0:d_ff + f0 + nf], preferred_element_type=_F32)
        part = jnp.dot((jax.nn.silu(g) * u).astype(_BF16), wout_ref[f0:f0 + nf, :], preferred_element_type=_F32)
        y = part if y is None else y + part
    return y


def _layer_kernel(*refs, kind, tiles_per_seq, final_norm, n_cast):
    mixer, n_par, halo = _MIXERS[kind]
    x_ref, mod_ref, gmix_ref = refs[:3]
    mixer_params = refs[3:3 + n_par]
    gffn_ref, win_ref, wout_ref, fg_ref = refs[3 + n_par:7 + n_par]
    cast_in = refs[7 + n_par:7 + n_par + n_cast]
    o_ref = refs[7 + n_par + n_cast]
    cast_out = refs[8 + n_par + n_cast:8 + n_par + 2 * n_cast]
    mixer_scratch = refs[8 + n_par + 2 * n_cast:]
    tm = x_ref.shape[0]

    for w_f32, w_bf16 in zip(cast_in, cast_out):
        w_bf16[...] = w_f32[...].astype(_BF16)

    t = pl.program_id(0)
    b = t // tiles_per_seq
    tile_in_seq = t % tiles_per_seq

    @pl.when(tile_in_seq == 0)
    def _():
        ext_ref = mixer_scratch[0]
        ext_ref[:, :halo, :] = jnp.zeros((ext_ref.shape[0], halo, V7X_LANES), _F32)

    shift2, scale2, gate2 = _mod_rows(mod_ref, (3, 4, 5), b)

    def ffn(r0, n, x1):
        hb = _modulated_rmsnorm(x1, gffn_ref[...], scale2, shift2).astype(_BF16)
        out = x1 + gate2 * _swiglu(hb, win_ref, wout_ref)
        if final_norm:
            ms = jnp.mean(out * out, axis=-1, keepdims=True)
            out = out * lax.rsqrt(ms + EPS) * fg_ref[...]
        o_ref[r0:r0 + n, :] = out

    mixer(x_ref, _mod_rows(mod_ref, (0, 1, 2), b), gmix_ref, mixer_params, mixer_scratch, tile_in_seq * tm, ffn)


def _layer_call(kind, name, xt, mod, layer, seq_len, mixer_params, ffn_params, mixer_scratch, final_norm,
                cast_jobs=()):
    tokens, d = xt.shape
    tm = TOKEN_TILE[kind]
    assert seq_len % tm == 0 and tm % ROW_CHAIN == 0 and ROW_CHAIN % ROW_BLOCK == 0 and d % V7X_LANES == 0
    steps = tokens // tm
    bsz = mod.shape[2]
    params = list(mixer_params) + list(ffn_params)
    x_spec = pl.BlockSpec((tm, d), lambda t: (t, 0))
    in_specs = [x_spec, _resident((None, N_MOD, bsz, d), (layer, 0, 0, 0))]
    in_specs += [_resident(shape, index) for _, shape, index in params]
    cast_views, cast_shapes, cast_out_specs = [], [], []
    for stack, idx in cast_jobs:
        rows, cols = math.prod(stack.shape[1:-1]), stack.shape[-1]
        block = next(r for r in range(V7X_BF16_SUBLANES, rows + 1, V7X_BF16_SUBLANES)
                     if rows % r == 0 and rows // r <= steps)
        n_blocks = rows // block
        cast_views.append(stack.reshape(stack.shape[0] * rows, cols))
        in_specs.append(pl.BlockSpec(
            (block, cols), lambda t, first=idx * n_blocks, last=n_blocks - 1: (first + jnp.minimum(t, last), 0)))
        cast_out_specs.append(pl.BlockSpec((block, cols), lambda t, last=n_blocks - 1: (jnp.minimum(t, last), 0)))
        cast_shapes.append(jax.ShapeDtypeStruct((rows, cols), _BF16))
    out, *casted = pl.pallas_call(
        functools.partial(_layer_kernel, kind=kind, tiles_per_seq=seq_len // tm, final_norm=final_norm,
                          n_cast=len(cast_jobs)),
        out_shape=[jax.ShapeDtypeStruct((tokens, d), xt.dtype)] + cast_shapes,
        grid=(steps,),
        in_specs=in_specs,
        out_specs=[x_spec] + cast_out_specs,
        scratch_shapes=list(mixer_scratch),
        compiler_params=pltpu.CompilerParams(
            dimension_semantics=("arbitrary",), vmem_limit_bytes=V7X_VMEM_LIMIT_BYTES),
        name=name,
    )(xt, mod, *[p for p, _, _ in params], *cast_views)
    return out, [w.reshape(stack.shape[1:]) for w, (stack, _) in zip(casted, cast_jobs)]


def _row_param(stacked, idx):
    n, d = stacked.shape
    return (stacked.reshape(n, 1, d), (None, 1, d), (idx, 0, 0))


def _stack_param(stacked, idx):
    return (stacked, (None,) + stacked.shape[1:], (idx,) + (0,) * (stacked.ndim - 1))


def _whole_param(array):
    return (array, array.shape, (0,) * array.ndim)


def _ext_scratch(tm, d, halo):
    return pltpu.VMEM((d // V7X_LANES, halo + tm, V7X_LANES), _F32)


def kernel(x, c, ada_w, ada_b, norm_mix_g, norm_ffn_g, a_w_in, a_conv_w, a_w_out, b_w_pw1, b_b_pw1, b_conv_w, b_conv_b, b_ln_g, b_ln_b, b_w_pw2, b_b_pw2, p_w_grp, p_scale, ffn_w_in, ffn_w_out, final_g):
    bsz, seq_len, d = x.shape
    depth = ada_w.shape[0]
    assert p_w_grp.shape[1] == len(POOL_WINDOWS)
    mod = _adaln(c, ada_w, ada_b)
    xt = x.reshape(bsz * seq_len, d)
    final_row = (final_g.reshape(1, 1, d), (None, 1, d), (0, 0, 0))

    def mxu_weights(i):
        kind, j = i % N_MIXERS, i // N_MIXERS
        mixer = ([(a_w_in, j), (a_w_out, j)], [(b_w_pw1, j), (b_w_pw2, j)], [(p_w_grp, j)])[kind]
        return mixer + [(ffn_w_in, i), (ffn_w_out, i)]

    later = [job for i in range(1, depth) for job in mxu_weights(i)]
    bf16_weights = {0: [stack[idx].astype(_BF16) for stack, idx in mxu_weights(0)]}
    for i in range(depth):
        kind, j = i % N_MIXERS, i // N_MIXERS
        tm = TOKEN_TILE[kind]
        tile_f32 = pltpu.VMEM((tm, d), _F32)
        *mixer_w, w_in, w_out = [_whole_param(w) for w in bf16_weights[i]]
        if kind == 0:
            name = "layer_short_conv"
            mixer_params = [_row_param(norm_mix_g, i), mixer_w[0], _stack_param(a_conv_w, j), mixer_w[1]]
            mixer_scratch = [_ext_scratch(tm, d, CONV_HALO_SHORT), tile_f32, tile_f32]
        elif kind == 1:
            name = "layer_conformer"
            mixer_params = [_row_param(norm_mix_g, i), mixer_w[0], _row_param(b_b_pw1, j),
                            _stack_param(b_conv_w, j), _row_param(b_conv_b, j), _row_param(b_ln_g, j),
                            _row_param(b_ln_b, j), mixer_w[1], _row_param(b_b_pw2, j)]
            mixer_scratch = [_ext_scratch(tm, d, CONV_HALO_LONG), tile_f32]
        else:
            name = "layer_pool"
            mixer_params = [_row_param(norm_mix_g, i), mixer_w[0], _row_param(p_scale, j)]
            mixer_scratch = [_ext_scratch(tm, d, POOL_HALO), tile_f32]
        ffn_params = [_row_param(norm_ffn_g, i), w_in, w_out, final_row]
        xt, casted = _layer_call(kind, name, xt, mod, i, seq_len, mixer_params, ffn_params, mixer_scratch,
                                 final_norm=(i == depth - 1), cast_jobs=later if i == 0 else ())
        if i == 0:
            for li in range(1, depth):
                n = len(mxu_weights(li))
                bf16_weights[li], casted = casted[:n], casted[n:]
    return xt.reshape(bsz, seq_len, d)
```

```python
import functools
import math

import jax
import jax.numpy as jnp
from jax import lax
from jax.experimental import pallas as pl
from jax.experimental.pallas import tpu as pltpu

EPS = 1e-6
POOL_WINDOWS = (2, 4, 8, 16)
N_MIXERS = 3
N_MOD = 6

V7X_LANES = 128
V7X_BF16_SUBLANES = 16
V7X_MXU_COLUMNS = 256
V7X_VMEM_LIMIT_BYTES = 56 * 1024 * 1024
ROW_CHAIN = 256
TOKEN_TILE = {0: 512, 1: 1024, 2: 1024}
ROW_BLOCK = 64
FFN_CHUNKS = 2
CONV_HALO_SHORT = 8
CONV_HALO_LONG = 32
POOL_HALO = 16

_F32 = jnp.float32
_BF16 = jnp.bfloat16


def _resident(block_shape, index):
    return pl.BlockSpec(block_shape, lambda t: index, pipeline_mode=pl.Buffered(1))


def _row_chains(tile_rows):
    return [(r0, ROW_CHAIN) for r0 in range(0, tile_rows, ROW_CHAIN)]


def _mod_rows(mod_ref, js, b):
    return [mod_ref[j, pl.ds(b, 1), :] for j in js]


def _modulated_rmsnorm(x, gain, scale, shift):
    ms = jnp.mean(x * x, axis=-1, keepdims=True)
    return (x * lax.rsqrt(ms + EPS)) * (gain * (1.0 + scale)) + shift


def _lanes(lb):
    return slice(lb * V7X_LANES, (lb + 1) * V7X_LANES)


def _to_lane_blocks(ext_ref, halo, r0, value):
    for lb in range(ext_ref.shape[0]):
        ext_ref[lb, halo + r0:halo + r0 + value.shape[0], :] = value[:, _lanes(lb)]


def _carry_halo(ext_ref, halo):
    tm = ext_ref.shape[1] - halo
    ext_ref[:, :halo, :] = ext_ref[:, tm:, :]


def _causal_dwconv(ext_ref, cw_ref, out_ref, halo, r0, n):
    taps = cw_ref.shape[0]
    first = halo - (taps - 1)
    for r in range(r0, r0 + n, ROW_BLOCK):
        for lb in range(ext_ref.shape[0]):
            acc = cw_ref[0:1, _lanes(lb)] * ext_ref[lb, r + first:r + first + ROW_BLOCK, :]
            for k in range(1, taps):
                acc = acc + cw_ref[k:k + 1, _lanes(lb)] * ext_ref[lb, r + first + k:r + first + k + ROW_BLOCK, :]
            out_ref[r:r + ROW_BLOCK, _lanes(lb)] = acc


def _trailing_pool(ext_ref, out_ref, halo, r0, n, seq_row0):
    lb_per_group = ext_ref.shape[0] // len(POOL_WINDOWS)
    for r in range(r0, r0 + n, ROW_BLOCK):
        pos1 = seq_row0 + (r + 1) + lax.broadcasted_iota(jnp.int32, (ROW_BLOCK, V7X_LANES), 0)
        for g, w in enumerate(POOL_WINDOWS):
            cnt = jnp.minimum(pos1, w).astype(_F32)
            for lb in range(g * lb_per_group, (g + 1) * lb_per_group):
                cur = ext_ref[lb, halo + r:halo + r + ROW_BLOCK, :]
                acc = cur
                for j in range(1, w):
                    acc = acc + ext_ref[lb, halo + r - j:halo + r - j + ROW_BLOCK, :]
                out_ref[r:r + ROW_BLOCK, _lanes(lb)] = acc / cnt - cur


def _adaln_kernel(c_ref, w_ref, b_ref, o_ref):
    c_act = jax.nn.silu(c_ref[...]).astype(_BF16)
    o_ref[...] = jnp.dot(c_act, w_ref[...].astype(_BF16), preferred_element_type=_F32) + b_ref[...]


def _adaln(c, ada_w, ada_b):
    depth, d, _ = ada_w.shape
    bsz = c.shape[0]
    return pl.pallas_call(
        _adaln_kernel,
        out_shape=jax.ShapeDtypeStruct((depth, N_MOD, bsz, d), _F32),
        grid=(depth, N_MOD),
        in_specs=[
            pl.BlockSpec((bsz, d), lambda i, j: (0, 0)),
            pl.BlockSpec((None, d, d), lambda i, j: (i, 0, j)),
            pl.BlockSpec((None, None, 1, d), lambda i, j: (i, j, 0, 0)),
        ],
        out_specs=pl.BlockSpec((None, None, bsz, d), lambda i, j: (i, j, 0, 0)),
        compiler_params=pltpu.CompilerParams(dimension_semantics=("arbitrary", "arbitrary")),
        name="adaln_mod",
    )(c, ada_w, ada_b.reshape(depth, N_MOD, 1, d))


def _short_conv_mixer(x_ref, mod, g_ref, params, scratch, seq_row0, emit):
    del seq_row0
    win_ref, cw_ref, wout_ref = params
    pext_ref, v_ref, gate_ref = scratch
    shift, scale, gate = mod
    tm, d = x_ref.shape
    for r0, n in _row_chains(tm):
        h = _modulated_rmsnorm(x_ref[r0:r0 + n, :], g_ref[...], scale, shift)
        proj = jnp.dot(h.astype(_BF16), win_ref[...], preferred_element_type=_F32)
        gate_ref[r0:r0 + n, :] = proj[:, :d]
        _to_lane_blocks(pext_ref, CONV_HALO_SHORT, r0, proj[:, d:2 * d] * proj[:, 2 * d:])
    for r0, n in _row_chains(tm):
        _causal_dwconv(pext_ref, cw_ref, v_ref, CONV_HALO_SHORT, r0, n)
        gated = gate_ref[r0:r0 + n, :] * v_ref[r0:r0 + n, :]
        y = jnp.dot(gated.astype(_BF16), wout_ref[...], preferred_element_type=_F32)
        emit(r0, n, x_ref[r0:r0 + n, :] + gate * y)
    _carry_halo(pext_ref, CONV_HALO_SHORT)


def _conformer_mixer(x_ref, mod, g_ref, params, scratch, seq_row0, emit):
    del seq_row0
    w1_ref, b1_ref, cw_ref, cb_ref, lng_ref, lnb_ref, w2_ref, b2_ref = params
    uext_ref, v_ref = scratch
    shift, scale, gate = mod
    tm, d = x_ref.shape
    for r0, n in _row_chains(tm):
        h = _modulated_rmsnorm(x_ref[r0:r0 + n, :], g_ref[...], scale, shift)
        p = jnp.dot(h.astype(_BF16), w1_ref[...], preferred_element_type=_F32) + b1_ref[...]
        _to_lane_blocks(uext_ref, CONV_HALO_LONG, r0, p[:, :d] * jax.nn.sigmoid(p[:, d:]))
    for r0, n in _row_chains(tm):
        _causal_dwconv(uext_ref, cw_ref, v_ref, CONV_HALO_LONG, r0, n)
        v = v_ref[r0:r0 + n, :] + cb_ref[...]
        mu = jnp.mean(v, axis=-1, keepdims=True)
        vc = v - mu
        var = jnp.mean(vc * vc, axis=-1, keepdims=True)
        z = jax.nn.silu(vc * lax.rsqrt(var + EPS) * lng_ref[...] + lnb_ref[...])
        y = jnp.dot(z.astype(_BF16), w2_ref[...], preferred_element_type=_F32) + b2_ref[...]
        emit(r0, n, x_ref[r0:r0 + n, :] + gate * y)
    _carry_halo(uext_ref, CONV_HALO_LONG)


def _pool_mixer(x_ref, mod, g_ref, params, scratch, seq_row0, emit):
    wg_ref, ps_ref = params
    hext_ref, pooled_ref = scratch
    shift, scale, gate = mod
    tm = x_ref.shape[0]
    grp = wg_ref.shape[1]
    for r0, n in _row_chains(tm):
        _to_lane_blocks(hext_ref, POOL_HALO, r0, _modulated_rmsnorm(x_ref[r0:r0 + n, :], g_ref[...], scale, shift))
    for r0, n in _row_chains(tm):
        _trailing_pool(hext_ref, pooled_ref, POOL_HALO, r0, n, seq_row0)
        pooled = pooled_ref[r0:r0 + n, :].astype(_BF16)
        y = jnp.concatenate(
            [jnp.dot(pooled[:, g * grp:(g + 1) * grp], wg_ref[g], preferred_element_type=_F32)
             for g in range(wg_ref.shape[0])], axis=-1) * ps_ref[...]
        emit(r0, n, x_ref[r0:r0 + n, :] + gate * y)
    _carry_halo(hext_ref, POOL_HALO)


_MIXERS = {
    0: (_short_conv_mixer, 3, CONV_HALO_SHORT),
    1: (_conformer_mixer, 8, CONV_HALO_LONG),
    2: (_pool_mixer, 2, POOL_HALO),
}


def _ffn_chunks(d_ff):
    tiles = d_ff // V7X_MXU_COLUMNS
    sizes = [(tiles // FFN_CHUNKS + (c < tiles % FFN_CHUNKS)) * V7X_MXU_COLUMNS for c in range(FFN_CHUNKS)]
    firsts = [sum(sizes[:c]) for c in range(FFN_CHUNKS)]
    return list(zip(firsts, sizes))


def _swiglu(hb, win_ref, wout_ref):
    d_ff = wout_ref.shape[0]
    y = None
    for f0, nf in _ffn_chunks(d_ff):
        g = jnp.dot(hb, win_ref[:, f0:f0 + nf], preferred_element_type=_F32)
        u = jnp.dot(hb, win_ref[:, d_ff + f0:d_ff + f0 + nf], preferred_element_type=_F32)
        part = jnp.dot((jax.nn.silu(g) * u).astype(_BF16), wout_ref[f0:f0 + nf, :], preferred_element_type=_F32)
        y = part if y is None else y + part
    return y


def _layer_kernel(*refs, kind, tiles_per_seq, final_norm, n_cast):
    mixer, n_par, halo = _MIXERS[kind]
    x_ref, mod_ref, gmix_ref = refs[:3]
    mixer_params = refs[3:3 + n_par]
    gffn_ref, win_ref, wout_ref, fg_ref = refs[3 + n_par:7 + n_par]
    cast_in = refs[7 + n_par:7 + n_par + n_cast]
    o_ref = refs[7 + n_par + n_cast]
    cast_out = refs[8 + n_par + n_cast:8 + n_par + 2 * n_cast]
    mixer_scratch = refs[8 + n_par + 2 * n_cast:]
    tm = x_ref.shape[0]

    for w_f32, w_bf16 in zip(cast_in, cast_out):
        w_bf16[...] = w_f32[...].astype(_BF16)

    t = pl.program_id(0)
    b = t // tiles_per_seq
    tile_in_seq = t % tiles_per_seq

    @pl.when(tile_in_seq == 0)
    def _():
        ext_ref = mixer_scratch[0]
        ext_ref[:, :halo, :] = jnp.zeros((ext_ref.shape[0], halo, V7X_LANES), _F32)

    shift2, scale2, gate2 = _mod_rows(mod_ref, (3, 4, 5), b)

    def ffn(r0, n, x1):
        hb = _modulated_rmsnorm(x1, gffn_ref[...], scale2, shift2).astype(_BF16)
        out = x1 + gate2 * _swiglu(hb, win_ref, wout_ref)
        if final_norm:
            ms = jnp.mean(out * out, axis=-1, keepdims=True)
            out = out * lax.rsqrt(ms + EPS) * fg_ref[...]
        o_ref[r0:r0 + n, :] = out

    mixer(x_ref, _mod_rows(mod_ref, (0, 1, 2), b), gmix_ref, mixer_params, mixer_scratch, tile_in_seq * tm, ffn)


def _layer_call(kind, name, xt, mod, layer, seq_len, mixer_params, ffn_params, mixer_scratch, final_norm,
                cast_jobs=()):
    tokens, d = xt.shape
    tm = TOKEN_TILE[kind]
    assert seq_len % tm == 0 and tm % ROW_CHAIN == 0 and ROW_CHAIN % ROW_BLOCK == 0 and d % V7X_LANES == 0
    steps = tokens // tm
    bsz = mod.shape[2]
    params = list(mixer_params) + list(ffn_params)
    x_spec = pl.BlockSpec((tm, d), lambda t: (t, 0))
    in_specs = [x_spec, _resident((None, N_MOD, bsz, d), (layer, 0, 0, 0))]
    in_specs += [_resident(shape, index) for _, shape, index in params]
    cast_views, cast_shapes, cast_out_specs = [], [], []
    for stack, idx in cast_jobs:
        rows, cols = math.prod(stack.shape[1:-1]), stack.shape[-1]
        block = next(r for r in range(V7X_BF16_SUBLANES, rows + 1, V7X_BF16_SUBLANES)
                     if rows % r == 0 and rows // r <= steps)
        n_blocks = rows // block
        cast_views.append(stack.reshape(stack.shape[0] * rows, cols))
        in_specs.append(pl.BlockSpec(
            (block, cols), lambda t, first=idx * n_blocks, last=n_blocks - 1: (first + jnp.minimum(t, last), 0)))
        cast_out_specs.append(pl.BlockSpec((block, cols), lambda t, last=n_blocks - 1: (jnp.minimum(t, last), 0)))
        cast_shapes.append(jax.ShapeDtypeStruct((rows, cols), _BF16))
    out, *casted = pl.pallas_call(
        functools.partial(_layer_kernel, kind=kind, tiles_per_seq=seq_len // tm, final_norm=final_norm,
                          n_cast=len(cast_jobs)),
        out_shape=[jax.ShapeDtypeStruct((tokens, d), xt.dtype)] + cast_shapes,
        grid=(steps,),
        in_specs=in_specs,
        out_specs=[x_spec] + cast_out_specs,
        scratch_shapes=list(mixer_scratch),
        compiler_params=pltpu.CompilerParams(
            dimension_semantics=("arbitrary",), vmem_limit_bytes=V7X_VMEM_LIMIT_BYTES),
        name=name,
    )(xt, mod, *[p for p, _, _ in params], *cast_views)
    return out, [w.reshape(stack.shape[1:]) for w, (stack, _) in zip(casted, cast_jobs)]


def _row_param(stacked, idx):
    n, d = stacked.shape
    return (stacked.reshape(n, 1, d), (None, 1, d), (idx, 0, 0))


def _stack_param(stacked, idx):
    return (stacked, (None,) + stacked.shape[1:], (idx,) + (0,) * (stacked.ndim - 1))


def _whole_param(array):
    return (array, array.shape, (0,) * array.ndim)


def _ext_scratch(tm, d, halo):
    return pltpu.VMEM((d // V7X_LANES, halo + tm, V7X_LANES), _F32)


def kernel(x, c, ada_w, ada_b, norm_mix_g, norm_ffn_g, a_w_in, a_conv_w, a_w_out, b_w_pw1, b_b_pw1, b_conv_w, b_conv_b, b_ln_g, b_ln_b, b_w_pw2, b_b_pw2, p_w_grp, p_scale, ffn_w_in, ffn_w_out, final_g):
    bsz, seq_len, d = x.shape
    depth = ada_w.shape[0]
    assert p_w_grp.shape[1] == len(POOL_WINDOWS)
    mod = _adaln(c, ada_w, ada_b)
    xt = x.reshape(bsz * seq_len, d)
    final_row = (final_g.reshape(1, 1, d), (None, 1, d), (0, 0, 0))

    def mxu_weights(i):
        kind, j = i % N_MIXERS, i // N_MIXERS
        mixer = ([(a_w_in, j), (a_w_out, j)], [(b_w_pw1, j), (b_w_pw2, j)], [(p_w_grp, j)])[kind]
        return mixer + [(ffn_w_in, i), (ffn_w_out, i)]

    later = [job for i in range(1, depth) for job in mxu_weights(i)]
    bf16_weights = {0: [stack[idx].astype(_BF16) for stack, idx in mxu_weights(0)]}
    for i in range(depth):
        kind, j = i % N_MIXERS, i // N_MIXERS
        tm = TOKEN_TILE[kind]
        tile_f32 = pltpu.VMEM((tm, d), _F32)
        *mixer_w, w_in, w_out = [_whole_param(w) for w in bf16_weights[i]]
        if kind == 0:
            name = "layer_short_conv"
            mixer_params = [_row_param(norm_mix_g, i), mixer_w[0], _stack_param(a_conv_w, j), mixer_w[1]]
            mixer_scratch = [_ext_scratch(tm, d, CONV_HALO_SHORT), tile_f32, tile_f32]
        elif kind == 1:
            name = "layer_conformer"
            mixer_params = [_row_param(norm_mix_g, i), mixer_w[0], _row_param(b_b_pw1, j),
                            _stack_param(b_conv_w, j), _row_param(b_conv_b, j), _row_param(b_ln_g, j),
                            _row_param(b_ln_b, j), mixer_w[1], _row_param(b_b_pw2, j)]
            mixer_scratch = [_ext_scratch(tm, d, CONV_HALO_LONG), tile_f32]
        else:
            name = "layer_pool"
            mixer_params = [_row_param(norm_mix_g, i), mixer_w[0], _row_param(p_scale, j)]
            mixer_scratch = [_ext_scratch(tm, d, POOL_HALO), tile_f32]
        ffn_params = [_row_param(norm_ffn_g, i), w_in, w_out, final_row]
        xt, casted = _layer_call(kind, name, xt, mod, i, seq_len, mixer_params, ffn_params, mixer_scratch,
                                 final_norm=(i == depth - 1), cast_jobs=later if i == 0 else ())
        if i == 0:
            for li in range(1, depth):
                n = len(mxu_weights(li))
                bf16_weights[li], casted = casted[:n], casted[n:]
    return xt.reshape(bsz, seq_len, d)
```

```python
import functools
import math

import jax
import jax.numpy as jnp
from jax import lax
from jax.experimental import pallas as pl
from jax.experimental.pallas import tpu as pltpu

EPS = 1e-6
POOL_WINDOWS = (2, 4, 8, 16)
N_MIXERS = 3
N_MOD = 6

V7X_LANES = 128
V7X_BF16_SUBLANES = 16
V7X_MXU_COLUMNS = 256
V7X_VMEM_LIMIT_BYTES = 56 * 1024 * 1024
ROW_CHAIN = {0: 256, 1: 128, 2: 256}
TOKEN_TILE = {0: 512, 1: 1024, 2: 1024}
ROW_BLOCK = 64
FFN_CHUNKS = 2
CONV_HALO_SHORT = 8
CONV_HALO_LONG = 32
POOL_HALO = 16

_F32 = jnp.float32
_BF16 = jnp.bfloat16


def _resident(block_shape, index):
    return pl.BlockSpec(block_shape, lambda t: index, pipeline_mode=pl.Buffered(1))


def _row_chains(tile_rows, kind):
    return [(r0, ROW_CHAIN[kind]) for r0 in range(0, tile_rows, ROW_CHAIN[kind])]


def _mod_rows(mod_ref, js, b):
    return [mod_ref[j, pl.ds(b, 1), :] for j in js]


def _modulated_rmsnorm(x, gain, scale, shift):
    ms = jnp.mean(x * x, axis=-1, keepdims=True)
    return (x * lax.rsqrt(ms + EPS)) * (gain * (1.0 + scale)) + shift


def _lanes(lb):
    return slice(lb * V7X_LANES, (lb + 1) * V7X_LANES)


def _to_lane_blocks(ext_ref, halo, r0, value):
    for lb in range(ext_ref.shape[0]):
        ext_ref[lb, halo + r0:halo + r0 + value.shape[0], :] = value[:, _lanes(lb)]


def _carry_halo(ext_ref, halo):
    tm = ext_ref.shape[1] - halo
    ext_ref[:, :halo, :] = ext_ref[:, tm:, :]


def _causal_dwconv(ext_ref, cw_ref, out_ref, halo, r0, n):
    taps = cw_ref.shape[0]
    first = halo - (taps - 1)
    for r in range(r0, r0 + n, ROW_BLOCK):
        for lb in range(ext_ref.shape[0]):
            acc = cw_ref[0:1, _lanes(lb)] * ext_ref[lb, r + first:r + first + ROW_BLOCK, :]
            for k in range(1, taps):
                acc = acc + cw_ref[k:k + 1, _lanes(lb)] * ext_ref[lb, r + first + k:r + first + k + ROW_BLOCK, :]
            out_ref[r:r + ROW_BLOCK, _lanes(lb)] = acc


def _trailing_pool(ext_ref, out_ref, halo, r0, n, seq_row0):
    lb_per_group = ext_ref.shape[0] // len(POOL_WINDOWS)
    for r in range(r0, r0 + n, ROW_BLOCK):
        pos1 = seq_row0 + (r + 1) + lax.broadcasted_iota(jnp.int32, (ROW_BLOCK, V7X_LANES), 0)
        for g, w in enumerate(POOL_WINDOWS):
            cnt = jnp.minimum(pos1, w).astype(_F32)
            for lb in range(g * lb_per_group, (g + 1) * lb_per_group):
                cur = ext_ref[lb, halo + r:halo + r + ROW_BLOCK, :]
                acc = cur
                for j in range(1, w):
                    acc = acc + ext_ref[lb, halo + r - j:halo + r - j + ROW_BLOCK, :]
                out_ref[r:r + ROW_BLOCK, _lanes(lb)] = acc / cnt - cur


def _adaln_kernel(c_ref, w_ref, b_ref, o_ref):
    c_act = jax.nn.silu(c_ref[...]).astype(_BF16)
    o_ref[...] = jnp.dot(c_act, w_ref[...].astype(_BF16), preferred_element_type=_F32) + b_ref[...]


def _adaln(c, ada_w, ada_b):
    depth, d, _ = ada_w.shape
    bsz = c.shape[0]
    return pl.pallas_call(
        _adaln_kernel,
        out_shape=jax.ShapeDtypeStruct((depth, N_MOD, bsz, d), _F32),
        grid=(depth, N_MOD),
        in_specs=[
            pl.BlockSpec((bsz, d), lambda i, j: (0, 0)),
            pl.BlockSpec((None, d, d), lambda i, j: (i, 0, j)),
            pl.BlockSpec((None, None, 1, d), lambda i, j: (i, j, 0, 0)),
        ],
        out_specs=pl.BlockSpec((None, None, bsz, d), lambda i, j: (i, j, 0, 0)),
        compiler_params=pltpu.CompilerParams(dimension_semantics=("arbitrary", "arbitrary")),
        name="adaln_mod",
    )(c, ada_w, ada_b.reshape(depth, N_MOD, 1, d))


def _short_conv_mixer(x_ref, mod, g_ref, params, scratch, seq_row0, emit):
    del seq_row0
    win_ref, cw_ref, wout_ref = params
    pext_ref, v_ref, gate_ref = scratch
    shift, scale, gate = mod
    tm, d = x_ref.shape
    for r0, n in _row_chains(tm, 0):
        h = _modulated_rmsnorm(x_ref[r0:r0 + n, :], g_ref[...], scale, shift)
        proj = jnp.dot(h.astype(_BF16), win_ref[...], preferred_element_type=_F32)
        gate_ref[r0:r0 + n, :] = proj[:, :d]
        _to_lane_blocks(pext_ref, CONV_HALO_SHORT, r0, proj[:, d:2 * d] * proj[:, 2 * d:])
    for r0, n in _row_chains(tm, 0):
        _causal_dwconv(pext_ref, cw_ref, v_ref, CONV_HALO_SHORT, r0, n)
        gated = gate_ref[r0:r0 + n, :] * v_ref[r0:r0 + n, :]
        y = jnp.dot(gated.astype(_BF16), wout_ref[...], preferred_element_type=_F32)
        emit(r0, n, x_ref[r0:r0 + n, :] + gate * y)
    _carry_halo(pext_ref, CONV_HALO_SHORT)


def _conformer_mixer(x_ref, mod, g_ref, params, scratch, seq_row0, emit):
    del seq_row0
    w1_ref, b1_ref, cw_ref, cb_ref, lng_ref, lnb_ref, w2_ref, b2_ref = params
    uext_ref, v_ref = scratch
    shift, scale, gate = mod
    tm, d = x_ref.shape
    for r0, n in _row_chains(tm, 1):
        h = _modulated_rmsnorm(x_ref[r0:r0 + n, :], g_ref[...], scale, shift)
        p = jnp.dot(h.astype(_BF16), w1_ref[...], preferred_element_type=_F32) + b1_ref[...]
        _to_lane_blocks(uext_ref, CONV_HALO_LONG, r0, p[:, :d] * jax.nn.sigmoid(p[:, d:]))
    for r0, n in _row_chains(tm, 1):
        _causal_dwconv(uext_ref, cw_ref, v_ref, CONV_HALO_LONG, r0, n)
        v = v_ref[r0:r0 + n, :] + cb_ref[...]
        mu = jnp.mean(v, axis=-1, keepdims=True)
        vc = v - mu
        var = jnp.mean(vc * vc, axis=-1, keepdims=True)
        z = jax.nn.silu(vc * lax.rsqrt(var + EPS) * lng_ref[...] + lnb_ref[...])
        y = jnp.dot(z.astype(_BF16), w2_ref[...], preferred_element_type=_F32) + b2_ref[...]
        emit(r0, n, x_ref[r0:r0 + n, :] + gate * y)
    _carry_halo(uext_ref, CONV_HALO_LONG)


def _pool_mixer(x_ref, mod, g_ref, params, scratch, seq_row0, emit):
    wg_ref, ps_ref = params
    hext_ref, pooled_ref = scratch
    shift, scale, gate = mod
    tm = x_ref.shape[0]
    grp = wg_ref.shape[1]
    for r0, n in _row_chains(tm, 2):
        _to_lane_blocks(hext_ref, POOL_HALO, r0, _modulated_rmsnorm(x_ref[r0:r0 + n, :], g_ref[...], scale, shift))
    for r0, n in _row_chains(tm, 2):
        _trailing_pool(hext_ref, pooled_ref, POOL_HALO, r0, n, seq_row0)
        pooled = pooled_ref[r0:r0 + n, :].astype(_BF16)
        y = jnp.concatenate(
            [jnp.dot(pooled[:, g * grp:(g + 1) * grp], wg_ref[g], preferred_element_type=_F32)
             for g in range(wg_ref.shape[0])], axis=-1) * ps_ref[...]
        emit(r0, n, x_ref[r0:r0 + n, :] + gate * y)
    _carry_halo(hext_ref, POOL_HALO)


_MIXERS = {
    0: (_short_conv_mixer, 3, CONV_HALO_SHORT),
    1: (_conformer_mixer, 8, CONV_HALO_LONG),
    2: (_pool_mixer, 2, POOL_HALO),
}


def _ffn_chunks(d_ff):
    tiles = d_ff // V7X_MXU_COLUMNS
    sizes = [(tiles // FFN_CHUNKS + (c < tiles % FFN_CHUNKS)) * V7X_MXU_COLUMNS for c in range(FFN_CHUNKS)]
    firsts = [sum(sizes[:c]) for c in range(FFN_CHUNKS)]
    return list(zip(firsts, sizes))


def _swiglu(hb, win_ref, wout_ref):
    d_ff = wout_ref.shape[0]
    y = None
    for f0, nf in _ffn_chunks(d_ff):
        g = jnp.dot(hb, win_ref[:, f0:f0 + nf], preferred_element_type=_F32)
        u = jnp.dot(hb, win_ref[:, d_ff + f0:d_ff + f0 + nf], preferred_element_type=_F32)
        part = jnp.dot((jax.nn.silu(g) * u).astype(_BF16), wout_ref[f0:f0 + nf, :], preferred_element_type=_F32)
        y = part if y is None else y + part
    return y


def _layer_kernel(*refs, kind, tiles_per_seq, final_norm, n_cast):
    mixer, n_par, halo = _MIXERS[kind]
    x_ref, mod_ref, gmix_ref = refs[:3]
    mixer_params = refs[3:3 + n_par]
    gffn_ref, win_ref, wout_ref, fg_ref = refs[3 + n_par:7 + n_par]
    cast_in = refs[7 + n_par:7 + n_par + n_cast]
    o_ref = refs[7 + n_par + n_cast]
    cast_out = refs[8 + n_par + n_cast:8 + n_par + 2 * n_cast]
    mixer_scratch = refs[8 + n_par + 2 * n_cast:]
    tm = x_ref.shape[0]

    for w_f32, w_bf16 in zip(cast_in, cast_out):
        w_bf16[...] = w_f32[...].astype(_BF16)

    t = pl.program_id(0)
    b = t // tiles_per_seq
    tile_in_seq = t % tiles_per_seq

    @pl.when(tile_in_seq == 0)
    def _():
        ext_ref = mixer_scratch[0]
        ext_ref[:, :halo, :] = jnp.zeros((ext_ref.shape[0], halo, V7X_LANES), _F32)

    shift2, scale2, gate2 = _mod_rows(mod_ref, (3, 4, 5), b)

    def ffn(r0, n, x1):
        hb = _modulated_rmsnorm(x1, gffn_ref[...], scale2, shift2).astype(_BF16)
        out = x1 + gate2 * _swiglu(hb, win_ref, wout_ref)
        if final_norm:
            ms = jnp.mean(out * out, axis=-1, keepdims=True)
            out = out * lax.rsqrt(ms + EPS) * fg_ref[...]
        o_ref[r0:r0 + n, :] = out

    mixer(x_ref, _mod_rows(mod_ref, (0, 1, 2), b), gmix_ref, mixer_params, mixer_scratch, tile_in_seq * tm, ffn)


def _layer_call(kind, name, xt, mod, layer, seq_len, mixer_params, ffn_params, mixer_scratch, final_norm,
                cast_jobs=()):
    tokens, d = xt.shape
    tm = TOKEN_TILE[kind]
    assert seq_len % tm == 0 and tm % ROW_CHAIN[kind] == 0 and ROW_CHAIN[kind] % ROW_BLOCK == 0 and d % V7X_LANES == 0
    steps = tokens // tm
    bsz = mod.shape[2]
    params = list(mixer_params) + list(ffn_params)
    x_spec = pl.BlockSpec((tm, d), lambda t: (t, 0))
    in_specs = [x_spec, _resident((None, N_MOD, bsz, d), (layer, 0, 0, 0))]
    in_specs += [_resident(shape, index) for _, shape, index in params]
    cast_views, cast_shapes, cast_out_specs = [], [], []
    for stack, idx in cast_jobs:
        rows, cols = math.prod(stack.shape[1:-1]), stack.shape[-1]
        block = next(r for r in range(V7X_BF16_SUBLANES, rows + 1, V7X_BF16_SUBLANES)
                     if rows % r == 0 and rows // r <= steps)
        n_blocks = rows // block
        cast_views.append(stack.reshape(stack.shape[0] * rows, cols))
        in_specs.append(pl.BlockSpec(
            (block, cols), lambda t, first=idx * n_blocks, last=n_blocks - 1: (first + jnp.minimum(t, last), 0)))
        cast_out_specs.append(pl.BlockSpec((block, cols), lambda t, last=n_blocks - 1: (jnp.minimum(t, last), 0)))
        cast_shapes.append(jax.ShapeDtypeStruct((rows, cols), _BF16))
    out, *casted = pl.pallas_call(
        functools.partial(_layer_kernel, kind=kind, tiles_per_seq=seq_len // tm, final_norm=final_norm,
                          n_cast=len(cast_jobs)),
        out_shape=[jax.ShapeDtypeStruct((tokens, d), xt.dtype)] + cast_shapes,
        grid=(steps,),
        in_specs=in_specs,
        out_specs=[x_spec] + cast_out_specs,
        scratch_shapes=list(mixer_scratch),
        compiler_params=pltpu.CompilerParams(
            dimension_semantics=("arbitrary",), vmem_limit_bytes=V7X_VMEM_LIMIT_BYTES),
        name=name,
    )(xt, mod, *[p for p, _, _ in params], *cast_views)
    return out, [w.reshape(stack.shape[1:]) for w, (stack, _) in zip(casted, cast_jobs)]


def _row_param(stacked, idx):
    n, d = stacked.shape
    return (stacked.reshape(n, 1, d), (None, 1, d), (idx, 0, 0))


def _stack_param(stacked, idx):
    return (stacked, (None,) + stacked.shape[1:], (idx,) + (0,) * (stacked.ndim - 1))


def _whole_param(array):
    return (array, array.shape, (0,) * array.ndim)


def _ext_scratch(tm, d, halo):
    return pltpu.VMEM((d // V7X_LANES, halo + tm, V7X_LANES), _F32)


def kernel(x, c, ada_w, ada_b, norm_mix_g, norm_ffn_g, a_w_in, a_conv_w, a_w_out, b_w_pw1, b_b_pw1, b_conv_w, b_conv_b, b_ln_g, b_ln_b, b_w_pw2, b_b_pw2, p_w_grp, p_scale, ffn_w_in, ffn_w_out, final_g):
    bsz, seq_len, d = x.shape
    depth = ada_w.shape[0]
    assert p_w_grp.shape[1] == len(POOL_WINDOWS)
    mod = _adaln(c, ada_w, ada_b)
    xt = x.reshape(bsz * seq_len, d)
    final_row = (final_g.reshape(1, 1, d), (None, 1, d), (0, 0, 0))

    def mxu_weights(i):
        kind, j = i % N_MIXERS, i // N_MIXERS
        mixer = ([(a_w_in, j), (a_w_out, j)], [(b_w_pw1, j), (b_w_pw2, j)], [(p_w_grp, j)])[kind]
        return mixer + [(ffn_w_in, i), (ffn_w_out, i)]

    later = [job for i in range(1, depth) for job in mxu_weights(i)]
    bf16_weights = {0: [stack[idx].astype(_BF16) for stack, idx in mxu_weights(0)]}
    for i in range(depth):
        kind, j = i % N_MIXERS, i // N_MIXERS
        tm = TOKEN_TILE[kind]
        tile_f32 = pltpu.VMEM((tm, d), _F32)
        *mixer_w, w_in, w_out = [_whole_param(w) for w in bf16_weights[i]]
        if kind == 0:
            name = "layer_short_conv"
            mixer_params = [_row_param(norm_mix_g, i), mixer_w[0], _stack_param(a_conv_w, j), mixer_w[1]]
            mixer_scratch = [_ext_scratch(tm, d, CONV_HALO_SHORT), tile_f32, tile_f32]
        elif kind == 1:
            name = "layer_conformer"
            mixer_params = [_row_param(norm_mix_g, i), mixer_w[0], _row_param(b_b_pw1, j),
                            _stack_param(b_conv_w, j), _row_param(b_conv_b, j), _row_param(b_ln_g, j),
                            _row_param(b_ln_b, j), mixer_w[1], _row_param(b_b_pw2, j)]
            mixer_scratch = [_ext_scratch(tm, d, CONV_HALO_LONG), tile_f32]
        else:
            name = "layer_pool"
            mixer_params = [_row_param(norm_mix_g, i), mixer_w[0], _row_param(p_scale, j)]
            mixer_scratch = [_ext_scratch(tm, d, POOL_HALO), tile_f32]
        ffn_params = [_row_param(norm_ffn_g, i), w_in, w_out, final_row]
        xt, casted = _layer_call(kind, name, xt, mod, i, seq_len, mixer_params, ffn_params, mixer_scratch,
                                 final_norm=(i == depth - 1), cast_jobs=later if i == 0 else ())
        if i == 0:
            for li in range(1, depth):
                n = len(mxu_weights(li))
                bf16_weights[li], casted = casted[:n], casted[n:]
    return xt.reshape(bsz, seq_len, d)
```
